```python
import jax, jax.numpy as jnp
from jax import lax
import numpy as np

D_MODEL = 1024
BATCH = 16
SEQ = 2048
DEPTH = 2

HEAD_DIM = 64
SB_HEADS = 8
SB_WIDTH = SB_HEADS * HEAD_DIM
CONV_CH = 512
CONV_WIDTH = 31
NSA_HEADS = 8
NSA_KV_HEADS = 2
NSA_GROUP = NSA_HEADS // NSA_KV_HEADS
NSA_WIDTH = NSA_HEADS * HEAD_DIM
KV_WIDTH = NSA_KV_HEADS * HEAD_DIM
CMP_BLOCK = 32
CMP_STRIDE = 16
SEL_BLOCK = 64
SEL_TOPK = 8
WINDOW = 512
GMLP_WIDTH = 512
GMLP_GROUPS = 4
GMLP_CHUNK = 128
D_FF = -(-8 * D_MODEL // (3 * 256)) * 256
Q_BLOCK = 128
SEL_Q_BLOCK = 64
ALPHA = (2 * DEPTH) ** 0.25
OUT_INIT = (8 * DEPTH) ** -0.25
LN_EPS = 1e-5
NEG = -1e30
FORCE_BONUS = 1e3

EVEN_SPLITS = (SB_WIDTH, SB_WIDTH, SB_WIDTH, CONV_CH, CONV_CH)
ODD_SPLITS = (NSA_WIDTH,) + (KV_WIDTH,) * 6 + (3 * NSA_HEADS, GMLP_WIDTH, GMLP_WIDTH)
EVEN_IN = sum(EVEN_SPLITS)
ODD_IN = sum(ODD_SPLITS)
MIX_OUT = SB_WIDTH + CONV_CH

kernel_name = "hybrid_stickbreak_conformer_nsa_gmlp_deepnorm"


def layer_norm(x, g, b):
    xf = x.astype(jnp.float32)
    mu = jnp.mean(xf, axis=-1, keepdims=True)
    var = jnp.mean(jnp.square(xf - mu), axis=-1, keepdims=True)
    y = (xf - mu) * lax.rsqrt(var + LN_EPS) * g.astype(jnp.float32) + b.astype(jnp.float32)
    return y.astype(x.dtype)


def split_cols(h, sizes):
    offs = [int(o) for o in np.cumsum(sizes)[:-1]]
    return jnp.split(h, offs, axis=-1)


def masked_softmax(s, mask):
    s = jnp.where(mask, s, NEG)
    p = jax.nn.softmax(s, axis=-1)
    return jnp.where(mask, p, 0.0)


def stick_breaking_attention(q, k, v):
    B, S, H, D = q.shape
    scale = D ** -0.5
    outs = []
    for i in range(S // Q_BLOCK):
        q0 = i * Q_BLOCK
        q1 = q0 + Q_BLOCK
        z = jnp.einsum('bqhd,bkhd->bhqk', q[:, q0:q1], k[:, :q1]).astype(jnp.float32) * scale
        tpos = q0 + jnp.arange(Q_BLOCK)
        kpos = jnp.arange(q1)
        past = kpos[None, :] < tpos[:, None]
        log_beta = jax.nn.log_sigmoid(z)
        log_rem = jnp.where(past, log_beta - z, 0.0)
        between = lax.cumsum(log_rem, axis=3, reverse=True) - log_rem
        w = jnp.where(past, jnp.exp(log_beta + between), 0.0)
        outs.append(jnp.einsum('bhqk,bkhd->bqhd', w.astype(v.dtype), v[:, :q1]))
    return jnp.concatenate(outs, axis=1)


def conformer_conv(a, gate, w_dw, b_dw, ln_g, ln_b):
    C = a.shape[-1]
    h = a * jax.nn.sigmoid(gate)
    hp = jnp.pad(h, ((0, 0), (CONV_WIDTH - 1, 0), (0, 0)))
    y = lax.conv_general_dilated(hp, w_dw[:, None, :].astype(h.dtype), window_strides=(1,),
                                 padding='VALID', dimension_numbers=('NWC', 'WIO', 'NWC'),
                                 feature_group_count=C) + b_dw
    y = layer_norm(y, ln_g, ln_b)
    return jax.nn.silu(y)


def compress_blocks(blocks, pos, w1, w2):
    B, n, L, G, D = blocks.shape
    h = blocks + pos[:, None, :]
    h = jnp.transpose(h, (0, 1, 3, 2, 4)).reshape(B, n, G, L * D)
    return jax.nn.gelu(h @ w1) @ w2


def nsa_compressed(q5, k, v, pos_k, w1k, w2k, pos_v, w1v, w2v):
    B, S, G, R, D = q5.shape
    scale = D ** -0.5
    n_cmp = (S - CMP_BLOCK) // CMP_STRIDE + 1
    idx = jnp.arange(n_cmp)[:, None] * CMP_STRIDE + jnp.arange(CMP_BLOCK)[None, :]
    kc = compress_blocks(k[:, idx], pos_k, w1k, w2k)
    vc = compress_blocks(v[:, idx], pos_v, w1v, w2v)
    s = jnp.einsum('bsgrd,bngd->bgrsn', q5, kc).astype(jnp.float32) * scale
    tpos = jnp.arange(S)
    cstart = jnp.arange(n_cmp) * CMP_STRIDE
    mask = (cstart + CMP_BLOCK - 1)[None, :] <= tpos[:, None]
    p = masked_softmax(s, mask)
    o = jnp.einsum('bgrsn,bngd->bsgrd', p.astype(vc.dtype), vc)
    n_slc = S // SEL_BLOCK
    sstart = jnp.arange(n_slc) * SEL_BLOCK
    overlap = ((cstart[:, None] < sstart[None, :] + SEL_BLOCK) &
               (cstart[:, None] + CMP_BLOCK > sstart[None, :])).astype(jnp.float32)
    imp = jnp.einsum('bgrsn,nj->bsgj', p, overlap)
    return o, imp


def select_blocks(imp):
    B, S, G, n_slc = imp.shape
    blk_t = jnp.arange(S) // SEL_BLOCK
    j = jnp.arange(n_slc)
    valid = j[None, :] <= blk_t[:, None]
    forced = (j[None, :] == 0) | (j[None, :] == blk_t[:, None]) | (j[None, :] == blk_t[:, None] - 1)
    bonus = jnp.where(forced, FORCE_BONUS, 0.0)
    score = jnp.where(valid[None, :, None, :], imp + bonus[None, :, None, :], NEG)
    k_eff = min(SEL_TOPK, n_slc)
    _, sel = lax.top_k(score, k_eff)
    return sel


def nsa_selected(q5, k, v, sel):
    B, S, G, R, D = q5.shape
    scale = D ** -0.5
    n_slc = S // SEL_BLOCK
    kk = sel.shape[-1]
    kb = jnp.transpose(k.reshape(B, n_slc, SEL_BLOCK, G, D), (0, 3, 1, 2, 4))
    vb = jnp.transpose(v.reshape(B, n_slc, SEL_BLOCK, G, D), (0, 3, 1, 2, 4))
    nc = S // SEL_Q_BLOCK
    qc = jnp.moveaxis(q5.reshape(B, nc, SEL_Q_BLOCK, G, R, D), 1, 0)
    ic = jnp.moveaxis(sel.reshape(B, nc, SEL_Q_BLOCK, G, kk), 1, 0)
    starts = jnp.arange(nc, dtype=jnp.int32) * SEL_Q_BLOCK
    bi = jnp.arange(B)[:, None, None, None]
    gi = jnp.arange(G)[None, None, :, None]

    def one(args):
        qi, idx, st = args
        kg = kb[bi, gi, idx]
        vg = vb[bi, gi, idx]
        tpos = st + jnp.arange(SEL_Q_BLOCK)
        kpos = idx[..., None] * SEL_BLOCK + jnp.arange(SEL_BLOCK)
        mask = (kpos <= tpos[None, :, None, None, None]).reshape(B, SEL_Q_BLOCK, G, kk * SEL_BLOCK)
        s = jnp.einsum('bqgrd,bqgkld->bqgrkl', qi, kg).astype(jnp.float32) * scale
        s = s.reshape(B, SEL_Q_BLOCK, G, R, kk * SEL_BLOCK)
        p = masked_softmax(s, mask[:, :, :, None, :])
        return jnp.einsum('bqgrm,bqgmd->bqgrd', p.astype(vg.dtype),
                          vg.reshape(B, SEL_Q_BLOCK, G, kk * SEL_BLOCK, D))

    o = lax.map(one, (qc, ic, starts))
    return jnp.moveaxis(o, 0, 1).reshape(B, S, G, R, D)


def nsa_window(q5, k, v):
    B, S, G, R, D = q5.shape
    scale = D ** -0.5
    nq = S // Q_BLOCK
    span = Q_BLOCK + WINDOW
    kp = jnp.pad(k, ((0, 0), (WINDOW, 0), (0, 0), (0, 0)))
    vp = jnp.pad(v, ((0, 0), (WINDOW, 0), (0, 0), (0, 0)))
    qb = jnp.moveaxis(q5.reshape(B, nq, Q_BLOCK, G, R, D), 1, 0)
    starts = jnp.arange(nq, dtype=jnp.int32) * Q_BLOCK

    def one(args):
        qi, st = args
        kb = lax.dynamic_slice_in_dim(kp, st, span, axis=1)
        vb = lax.dynamic_slice_in_dim(vp, st, span, axis=1)
        tpos = st + jnp.arange(Q_BLOCK)
        kpos = st - WINDOW + jnp.arange(span)
        mask = ((kpos[None, :] >= 0) & (kpos[None, :] <= tpos[:, None]) &
                (tpos[:, None] - kpos[None, :] < WINDOW))
        s = jnp.einsum('bqgrd,bkgd->bgrqk', qi, kb).astype(jnp.float32) * scale
        p = masked_softmax(s, mask)
        return jnp.einsum('bgrqk,bkgd->bqgrd', p.astype(vb.dtype), vb)

    o = lax.map(one, (qb, starts))
    return jnp.moveaxis(o, 0, 1).reshape(B, S, G, R, D)


def chunked_gmlp(u, v, ln_g, ln_b, ws, bs):
    u = jax.nn.gelu(u)
    v = layer_norm(jax.nn.gelu(v), ln_g, ln_b)
    B, S, C = v.shape
    nch = S // GMLP_CHUNK
    cg = C // GMLP_GROUPS
    vr = v.reshape(B, nch, GMLP_CHUNK, GMLP_GROUPS, cg)
    tril = jnp.tril(jnp.ones((GMLP_CHUNK, GMLP_CHUNK), ws.dtype))
    mixed = jnp.einsum('gts,bcsgd->bctgd', ws * tril, vr) + jnp.transpose(bs)[None, None, :, :, None]
    return u * mixed.reshape(B, S, C)


def even_mixer(x, w_in, conv_w, conv_b, conv_ln_g, conv_ln_b, w_out):
    B, S, _ = x.shape
    q, k, v, a, g = split_cols(x @ w_in, EVEN_SPLITS)
    hs = (B, S, SB_HEADS, HEAD_DIM)
    o_sb = stick_breaking_attention(q.reshape(hs), k.reshape(hs), v.reshape(hs)).reshape(B, S, SB_WIDTH)
    o_cv = conformer_conv(a, g, conv_w, conv_b, conv_ln_g, conv_ln_b)
    return jnp.concatenate([o_sb, o_cv], axis=-1) @ w_out


def odd_mixer(x, w_in, cmpk_pos, cmpk_w1, cmpk_w2, cmpv_pos, cmpv_w1, cmpv_w2,
              gmlp_ln_g, gmlp_ln_b, gmlp_ws, gmlp_bs, w_out):
    B, S, _ = x.shape
    q, kc, vc, ks, vs, kw, vw, gt, u, v = split_cols(x @ w_in, ODD_SPLITS)
    q5 = q.reshape(B, S, NSA_KV_HEADS, NSA_GROUP, HEAD_DIM)
    kvs = (B, S, NSA_KV_HEADS, HEAD_DIM)
    o_cmp, imp = nsa_compressed(q5, kc.reshape(kvs), vc.reshape(kvs),
                                cmpk_pos, cmpk_w1, cmpk_w2, cmpv_pos, cmpv_w1, cmpv_w2)
    sel = select_blocks(imp)
    o_slc = nsa_selected(q5, ks.reshape(kvs), vs.reshape(kvs), sel)
    o_win = nsa_window(q5, kw.reshape(kvs), vw.reshape(kvs))
    gates = jax.nn.sigmoid(gt).reshape(B, S, NSA_KV_HEADS, NSA_GROUP, 3)
    o_nsa = (gates[..., 0:1] * o_cmp + gates[..., 1:2] * o_slc + gates[..., 2:3] * o_win).reshape(B, S, NSA_WIDTH)
    o_mlp = chunked_gmlp(u, v, gmlp_ln_g, gmlp_ln_b, gmlp_ws, gmlp_bs)
    return jnp.concatenate([o_nsa, o_mlp], axis=-1) @ w_out


def swiglu(x, w_gate, w_up, w_down):
    return (jax.nn.silu(x @ w_gate) * (x @ w_up)) @ w_down


def setup_inputs(seed: int = 0) -> dict:
    key = jax.random.key(seed)
    ks = jax.random.split(key, 32)
    ne = (DEPTH + 1) // 2
    no = DEPTH // 2

    def nrm(k, shape, scale):
        return jax.random.normal(k, shape, jnp.float32) * scale

    L = CMP_BLOCK
    return {
        "x": nrm(ks[0], (BATCH, SEQ, D_MODEL), 1.0),
        "ev_w_in": nrm(ks[1], (ne, D_MODEL, EVEN_IN), D_MODEL ** -0.5),
        "ev_conv_w": nrm(ks[2], (ne, CONV_WIDTH, CONV_CH), CONV_WIDTH ** -0.5),
        "ev_conv_b": nrm(ks[3], (ne, CONV_CH), 0.02),
        "ev_conv_ln_g": 1.0 + nrm(ks[4], (ne, CONV_CH), 0.02),
        "ev_conv_ln_b": nrm(ks[5], (ne, CONV_CH), 0.02),
        "ev_w_out": nrm(ks[6], (ne, MIX_OUT, D_MODEL), MIX_OUT ** -0.5 * OUT_INIT),
        "od_w_in": nrm(ks[7], (no, D_MODEL, ODD_IN), D_MODEL ** -0.5),
        "od_cmpk_pos": nrm(ks[8], (no, L, HEAD_DIM), 0.02),
        "od_cmpk_w1": nrm(ks[9], (no, L * HEAD_DIM, HEAD_DIM), (L * HEAD_DIM) ** -0.5),
        "od_cmpk_w2": nrm(ks[10], (no, HEAD_DIM, HEAD_DIM), HEAD_DIM ** -0.5),
        "od_cmpv_pos": nrm(ks[11], (no, L, HEAD_DIM), 0.02),
        "od_cmpv_w1": nrm(ks[12], (no, L * HEAD_DIM, HEAD_DIM), (L * HEAD_DIM) ** -0.5),
        "od_cmpv_w2": nrm(ks[13], (no, HEAD_DIM, HEAD_DIM), HEAD_DIM ** -0.5),
        "od_gmlp_ln_g": 1.0 + nrm(ks[14], (no, GMLP_WIDTH), 0.02),
        "od_gmlp_ln_b": nrm(ks[15], (no, GMLP_WIDTH), 0.02),
        "od_gmlp_ws": nrm(ks[16], (no, GMLP_GROUPS, GMLP_CHUNK, GMLP_CHUNK), GMLP_CHUNK ** -0.5),
        "od_gmlp_bs": 1.0 + nrm(ks[17], (no, GMLP_GROUPS, GMLP_CHUNK), 0.1),
        "od_w_out": nrm(ks[18], (no, MIX_OUT, D_MODEL), MIX_OUT ** -0.5 * OUT_INIT),
        "ffn_w_gate": nrm(ks[19], (DEPTH, D_MODEL, D_FF), D_MODEL ** -0.5),
        "ffn_w_up": nrm(ks[20], (DEPTH, D_MODEL, D_FF), D_MODEL ** -0.5),
        "ffn_w_down": nrm(ks[21], (DEPTH, D_FF, D_MODEL), D_FF ** -0.5 * OUT_INIT),
        "ln1_g": 1.0 + nrm(ks[22], (DEPTH, D_MODEL), 0.02),
        "ln1_b": nrm(ks[23], (DEPTH, D_MODEL), 0.02),
        "ln2_g": 1.0 + nrm(ks[24], (DEPTH, D_MODEL), 0.02),
        "ln2_b": nrm(ks[25], (DEPTH, D_MODEL), 0.02),
    }


def reference(x, ev_w_in, ev_conv_w, ev_conv_b, ev_conv_ln_g, ev_conv_ln_b, ev_w_out,
              od_w_in, od_cmpk_pos, od_cmpk_w1, od_cmpk_w2, od_cmpv_pos, od_cmpv_w1, od_cmpv_w2,
              od_gmlp_ln_g, od_gmlp_ln_b, od_gmlp_ws, od_gmlp_bs, od_w_out,
              ffn_w_gate, ffn_w_up, ffn_w_down, ln1_g, ln1_b, ln2_g, ln2_b):
    for layer in range(DEPTH):
        i = layer // 2
        if layer % 2 == 0:
            m = even_mixer(x, ev_w_in[i], ev_conv_w[i], ev_conv_b[i], ev_conv_ln_g[i],
                           ev_conv_ln_b[i], ev_w_out[i])
        else:
            m = odd_mixer(x, od_w_in[i], od_cmpk_pos[i], od_cmpk_w1[i], od_cmpk_w2[i],
                          od_cmpv_pos[i], od_cmpv_w1[i], od_cmpv_w2[i], od_gmlp_ln_g[i],
                          od_gmlp_ln_b[i], od_gmlp_ws[i], od_gmlp_bs[i], od_w_out[i])
        x = layer_norm(ALPHA * x + m, ln1_g[layer], ln1_b[layer])
        x = layer_norm(ALPHA * x + swiglu(x, ffn_w_gate[layer], ffn_w_up[layer], ffn_w_down[layer]),
                       ln2_g[layer], ln2_b[layer])
    return x
```

```python
import functools

import numpy as np
import jax
import jax.numpy as jnp
from jax import lax
from jax.experimental import pallas as pl
from jax.experimental.pallas import tpu as pltpu

F32 = jnp.float32
BF16 = jnp.bfloat16

D_MODEL = 1024
DEPTH = 2
HEAD_DIM = 64
LANES = 128
SB_WIDTH = 512
CONV_CH = 512
CONV_WIDTH = 31
NSA_WIDTH = 512
KV_WIDTH = 128
CMP_BLOCK = 32
CMP_STRIDE = 16
SEL_BLOCK = 64
SEL_TOPK = 8
SEL_LANES = 32
WINDOW = 512
GMLP_WIDTH = 512
GMLP_GROUPS = 4
GMLP_CHUNK = 128
D_FF = 2816
ALPHA = (2 * DEPTH) ** 0.25
LN_EPS = 1e-5
NEG = -1e30
FORCE_BONUS = 1e3
QK_SCALE = HEAD_DIM ** -0.5

VMEM_LIMIT = 56 * 1024 * 1024


def _cparams(*sem):
    return pltpu.CompilerParams(dimension_semantics=sem, vmem_limit_bytes=VMEM_LIMIT)


def _layer_norm(y, g, b):
    mu = jnp.mean(y, axis=-1, keepdims=True)
    d = y - mu
    var = jnp.mean(d * d, axis=-1, keepdims=True)
    return d * lax.rsqrt(var + LN_EPS) * g + b


def _sigmoid(x):
    return 1.0 / (1.0 + jnp.exp(-x))


def _gelu_tanh(x):
    c = np.float32(np.sqrt(2.0 / np.pi))
    return x * (0.5 * (1.0 + jnp.tanh(c * (x + 0.044715 * (x * x * x)))))


def _dot(a, b):
    return jnp.dot(a, b, preferred_element_type=F32)


def _dot_nt(a, b):
    return lax.dot_general(a, b, (((1,), (1,)), ((), ())), preferred_element_type=F32)


def _proj_kernel(x_ref, w_ref, *out_refs, splits):
    xb = x_ref[...].astype(BF16)
    for (start, width), o_ref in zip(splits, out_refs):
        o_ref[...] = _dot(xb, w_ref[:, start:start + width]).astype(o_ref.dtype)


def _proj(x, w, splits, dtypes, tm=512):
    m, k = x.shape
    n = w.shape[1]
    return pl.pallas_call(
        functools.partial(_proj_kernel, splits=tuple(splits)),
        out_shape=[jax.ShapeDtypeStruct((m, wd), dt) for (_, wd), dt in zip(splits, dtypes)],
        grid=(m // tm,),
        in_specs=[pl.BlockSpec((tm, k), lambda i: (i, 0)),
                  pl.BlockSpec((k, n), lambda i: (0, 0))],
        out_specs=[pl.BlockSpec((tm, wd), lambda i: (i, 0)) for (_, wd) in splits],
        compiler_params=_cparams("parallel"),
        name="proj_in",
    )(x, w)


def _sb_kernel(q_ref, k_ref, v_ref, o_ref, *, tq, tk):
    i = pl.program_id(2)
    lane = lax.broadcasted_iota(jnp.int32, (1, LANES), 1)
    q2 = q_ref[0]
    rr = lax.broadcasted_iota(jnp.int32, (tk, tk), 0)
    cc = lax.broadcasted_iota(jnp.int32, (tk, tk), 1)
    upper = jnp.where(rr > cc, 1.0, 0.0).astype(BF16)
    tpos = i * tq + lax.broadcasted_iota(jnp.int32, (tq, 1), 0)
    kidx = lax.broadcasted_iota(jnp.int32, (1, tk), 1)
    qh = [q2 * jnp.where(lane // HEAD_DIM == h, QK_SCALE, 0.0).astype(BF16) for h in range(2)]
    nk = (i + 1) * (tq // tk)

    def body(it, state):
        j = nk - 1 - it
        r0 = pl.multiple_of(j * tk, tk)
        kt = k_ref[0, pl.ds(r0, tk), :]
        vt = v_ref[0, pl.ds(r0, tk), :]
        past = (r0 + kidx) < tpos
        new = []
        for h in range(2):
            carry, acc = state[h]
            z = _dot_nt(qh[h], kt)
            sp = jnp.maximum(z, 0.0) + jnp.log(1.0 + jnp.exp(-jnp.abs(z)))
            lr = jnp.where(past, -sp, 0.0)
            hi = lr.astype(BF16)
            lo = (lr - hi.astype(F32)).astype(BF16)
            between = _dot(hi, upper) + _dot(lo, upper) + carry
            w = jnp.where(past, jnp.exp(z - sp + between), 0.0)
            acc = acc + _dot(w.astype(BF16), vt)
            carry = carry + jnp.sum(lr, axis=1, keepdims=True)
            new.append((carry, acc))
        return tuple(new)

    zero = (jnp.zeros((tq, 1), F32), jnp.zeros((tq, LANES), F32))
    state = lax.fori_loop(0, nk, body, (zero, zero))
    o_ref[0] = jnp.where(lane < HEAD_DIM, state[0][1], state[1][1]).astype(o_ref.dtype)


def _sb_attention(qkv, tq=128, tk=128):
    b, s, _ = qkv.shape
    nhp = SB_WIDTH // LANES
    return pl.pallas_call(
        functools.partial(_sb_kernel, tq=tq, tk=tk),
        out_shape=jax.ShapeDtypeStruct((b, s, SB_WIDTH), BF16),
        grid=(b, nhp, s // tq),
        in_specs=[pl.BlockSpec((1, tq, LANES), lambda bi, hp, i: (bi, i, hp)),
                  pl.BlockSpec((1, s, LANES), lambda bi, hp, i: (bi, 0, nhp + hp)),
                  pl.BlockSpec((1, s, LANES), lambda bi, hp, i: (bi, 0, 2 * nhp + hp))],
        out_specs=pl.BlockSpec((1, tq, LANES), lambda bi, hp, i: (bi, i, hp)),
        compiler_params=_cparams("parallel", "parallel", "arbitrary"),
        name="sb_attention",
    )(qkv, qkv, qkv)


CONV_PAD = 32


def _conv_kernel(a_ref, g_ref, w_ref, b_ref, lg_ref, lb_ref, o_ref, hbuf, *, tc, sub):
    c = pl.program_id(1)

    @pl.when(c == 0)
    def _():
        hbuf[0:CONV_PAD, :] = jnp.zeros((CONV_PAD, CONV_CH), F32)

    hbuf[CONV_PAD:CONV_PAD + tc, :] = a_ref[0] * _sigmoid(g_ref[0])
    off = CONV_PAD - (CONV_WIDTH - 1)
    for r0 in range(0, tc, sub):
        acc = jnp.zeros((sub, CONV_CH), F32) + b_ref[...]
        for w in range(CONV_WIDTH):
            acc = acc + w_ref[w:w + 1, :] * hbuf[r0 + off + w:r0 + off + w + sub, :]
        y = _layer_norm(acc, lg_ref[...], lb_ref[...])
        o_ref[0, r0:r0 + sub, :] = (y * _sigmoid(y)).astype(o_ref.dtype)
    hbuf[0:CONV_PAD, :] = hbuf[tc:tc + CONV_PAD, :]


def _conformer_conv(ag, w_dw, b_dw, ln_g, ln_b, tc=256, sub=64):
    b, s, _ = ag.shape
    nblk = CONV_CH // CONV_CH
    vec = lambda v: v.reshape(1, CONV_CH).astype(F32)
    return pl.pallas_call(
        functools.partial(_conv_kernel, tc=tc, sub=sub),
        out_shape=jax.ShapeDtypeStruct((b, s, CONV_CH), BF16),
        grid=(b, s // tc),
        in_specs=[pl.BlockSpec((1, tc, CONV_CH), lambda bi, c: (bi, c, 0)),
                  pl.BlockSpec((1, tc, CONV_CH), lambda bi, c: (bi, c, nblk)),
                  pl.BlockSpec((CONV_WIDTH, CONV_CH), lambda bi, c: (0, 0)),
                  pl.BlockSpec((1, CONV_CH), lambda bi, c: (0, 0)),
                  pl.BlockSpec((1, CONV_CH), lambda bi, c: (0, 0)),
                  pl.BlockSpec((1, CONV_CH), lambda bi, c: (0, 0))],
        out_specs=pl.BlockSpec((1, tc, CONV_CH), lambda bi, c: (bi, c, 0)),
        scratch_shapes=[pltpu.VMEM((CONV_PAD + tc, CONV_CH), F32)],
        compiler_params=_cparams("parallel", "arbitrary"),
        name="conformer_conv",
    )(ag, ag, w_dw.astype(F32), vec(b_dw), vec(ln_g), vec(ln_b))


def _out_kernel(l_ref, r_ref, w_ref, x_ref, g_ref, b_ref, o_ref):
    kl = l_ref.shape[1]
    m = _dot(l_ref[...], w_ref[0:kl, :]) + _dot(r_ref[...], w_ref[kl:, :])
    o_ref[...] = _layer_norm(ALPHA * x_ref[...] + m, g_ref[...], b_ref[...])


def _out_proj_ln(left, right, w, x, g, b, tm=512):
    m, d = x.shape
    kl, kr = left.shape[1], right.shape[1]
    return pl.pallas_call(
        _out_kernel,
        out_shape=jax.ShapeDtypeStruct((m, d), F32),
        grid=(m // tm,),
        in_specs=[pl.BlockSpec((tm, kl), lambda i: (i, 0)),
                  pl.BlockSpec((tm, kr), lambda i: (i, 0)),
                  pl.BlockSpec((kl + kr, d), lambda i: (0, 0)),
                  pl.BlockSpec((tm, d), lambda i: (i, 0)),
                  pl.BlockSpec((1, d), lambda i: (0, 0)),
                  pl.BlockSpec((1, d), lambda i: (0, 0))],
        out_specs=pl.BlockSpec((tm, d), lambda i: (i, 0)),
        compiler_params=_cparams("parallel"),
        name="out_proj_ln",
    )(left, right, w, x, g.reshape(1, d).astype(F32), b.reshape(1, d).astype(F32))


FF_CHUNK = 256


def _ffn_kernel(x_ref, wg_ref, wu_ref, wd_ref, g_ref, b_ref, o_ref):
    x = x_ref[...]
    xb = x.astype(BF16)
    acc = jnp.zeros(x.shape, F32)
    for c0 in range(0, D_FF, FF_CHUNK):
        gate = _dot(xb, wg_ref[:, c0:c0 + FF_CHUNK])
        up = _dot(xb, wu_ref[:, c0:c0 + FF_CHUNK])
        h = (gate * _sigmoid(gate) * up).astype(BF16)
        acc = acc + _dot(h, wd_ref[c0:c0 + FF_CHUNK, :])
    o_ref[...] = _layer_norm(ALPHA * x + acc, g_ref[...], b_ref[...])


def _ffn_ln(x, wg, wu, wd, g, b, tm=512):
    m, d = x.shape
    ff = wg.shape[1]
    return pl.pallas_call(
        _ffn_kernel,
        out_shape=jax.ShapeDtypeStruct((m, d), F32),
        grid=(m // tm,),
        in_specs=[pl.BlockSpec((tm, d), lambda i: (i, 0)),
                  pl.BlockSpec((d, ff), lambda i: (0, 0), pipeline_mode=pl.Buffered(1)),
                  pl.BlockSpec((d, ff), lambda i: (0, 0), pipeline_mode=pl.Buffered(1)),
                  pl.BlockSpec((ff, d), lambda i: (0, 0), pipeline_mode=pl.Buffered(1)),
                  pl.BlockSpec((1, d), lambda i: (0, 0)),
                  pl.BlockSpec((1, d), lambda i: (0, 0))],
        out_specs=pl.BlockSpec((tm, d), lambda i: (i, 0)),
        compiler_params=_cparams("parallel"),
        name="ffn_ln",
    )(x, wg, wu, wd, g.reshape(1, d).astype(F32), b.reshape(1, d).astype(F32))


def _compress_kernel(xk_ref, xv_ref, pk_ref, pv_ref, w1k_ref, w1v_ref, w2k_ref, w2v_ref, ok_ref, ov_ref):
    half = (CMP_BLOCK // 2) * HEAD_DIM
    n = xk_ref.shape[2]
    for x_ref, p_ref, w1_ref, w2_ref, o_ref in ((xk_ref, pk_ref, w1k_ref, w2k_ref, ok_ref),
                                                (xv_ref, pv_ref, w1v_ref, w2v_ref, ov_ref)):
        for g in range(2):
            x = x_ref[0, g]
            top = _dot((x + p_ref[0:1, :]).astype(BF16), w1_ref[0:half, :])
            bot = _dot((x + p_ref[1:2, :]).astype(BF16), w1_ref[half:, :])
            h = top + pltpu.roll(bot, n - 1, 0)
            o_ref[0, g] = _dot(_gelu_tanh(h).astype(BF16), w2_ref[...])


def _compress(xk, xv, pos_k, pos_v, w1k, w1v, w2k, w2v):
    b, _, n, half = xk.shape
    xspec = pl.BlockSpec((1, 2, n, half), lambda bi: (bi, 0, 0, 0))
    full = lambda a: pl.BlockSpec(a.shape, lambda bi: (0,) * a.ndim)
    ospec = pl.BlockSpec((1, 2, n, HEAD_DIM), lambda bi: (bi, 0, 0, 0))
    args = (xk, xv, pos_k.reshape(2, half).astype(F32), pos_v.reshape(2, half).astype(F32),
            w1k.astype(BF16), w1v.astype(BF16), w2k.astype(BF16), w2v.astype(BF16))
    return pl.pallas_call(
        _compress_kernel,
        out_shape=[jax.ShapeDtypeStruct((b, 2, n, HEAD_DIM), F32)] * 2,
        grid=(b,),
        in_specs=[xspec, xspec] + [full(a) for a in args[2:]],
        out_specs=[ospec, ospec],
        compiler_params=_cparams("parallel"),
        name="nsa_compress",
    )(*args)


def _cmp_attn_kernel(q_ref, kc_ref, vc_ref, o_ref, sel_ref, *, tq, n_cmp):
    i = pl.program_id(1)
    n_slc = SEL_LANES
    lane = lax.broadcasted_iota(jnp.int32, (1, LANES), 1)
    tpos = i * tq + lax.broadcasted_iota(jnp.int32, (tq, 1), 0)
    valid = (lane * CMP_STRIDE + (CMP_BLOCK - 1) <= tpos) & (lane < n_cmp)
    kc = kc_ref[0]
    vc = vc_ref[0]
    nrow = lax.broadcasted_iota(jnp.int32, (LANES, LANES), 0)
    lcol = lax.broadcasted_iota(jnp.int32, (LANES, LANES), 1)
    jcol = lcol % n_slc
    ovl = ((nrow * CMP_STRIDE < jcol * SEL_BLOCK + SEL_BLOCK) &
           (nrow * CMP_STRIDE + CMP_BLOCK > jcol * SEL_BLOCK) & (nrow < n_cmp))
    imp = jnp.zeros((tq, LANES), F32)
    outs = [[None, None] for _ in range(4)]
    for g in range(2):
        hm = jnp.where(lane // HEAD_DIM == g, QK_SCALE, 0.0).astype(BF16)
        psum = jnp.zeros((tq, LANES), F32)
        for m in range(4):
            s = _dot_nt(q_ref[0, :, m * LANES:(m + 1) * LANES] * hm, kc)
            s = jnp.where(valid, s, NEG)
            e = jnp.where(valid, jnp.exp(s - jnp.max(s, axis=1, keepdims=True)), 0.0)
            l = jnp.sum(e, axis=1, keepdims=True)
            p = e / jnp.where(l > 0.0, l, 1.0)
            outs[m][g] = _dot(p.astype(BF16), vc)
            psum = psum + p
        ov_g = jnp.where(ovl & (lcol // n_slc == g), 1.0, 0.0).astype(BF16)
        hi = psum.astype(BF16)
        lo = (psum - hi.astype(F32)).astype(BF16)
        imp = imp + _dot(hi, ov_g) + _dot(lo, ov_g)
    for m in range(4):
        o_ref[0, :, m * LANES:(m + 1) * LANES] = jnp.where(lane < HEAD_DIM, outs[m][0], outs[m][1])

    jl = lane % n_slc
    blk_t = tpos // SEL_BLOCK
    in_range = lane < 2 * n_slc
    forced = (jl == 0) | (jl == blk_t) | (jl == blk_t - 1)
    score = jnp.where((jl <= blk_t) & in_range, imp + jnp.where(forced, FORCE_BONUS, 0.0), NEG)
    cnt = jnp.zeros((tq, LANES), F32)
    for c in range(n_slc):
        col = jnp.where(lane < n_slc, score[:, c:c + 1], score[:, n_slc + c:n_slc + c + 1])
        beats = (col > score) | ((col == score) & (jl > c))
        cnt = cnt + jnp.where(beats, 1.0, 0.0)
    sel_ref[0] = jnp.where((cnt < SEL_TOPK) & in_range, 1.0, 0.0).astype(sel_ref.dtype)


def _cmp_attention(qk, kc2, vc2, n_cmp, tq=256):
    b, s, _ = qk.shape
    return pl.pallas_call(
        functools.partial(_cmp_attn_kernel, tq=tq, n_cmp=n_cmp),
        out_shape=[jax.ShapeDtypeStruct((b, s, NSA_WIDTH), F32),
                   jax.ShapeDtypeStruct((b, s, LANES), BF16)],
        grid=(b, s // tq),
        in_specs=[pl.BlockSpec((1, tq, NSA_WIDTH), lambda bi, i: (bi, i, 0)),
                  pl.BlockSpec((1, LANES, LANES), lambda bi, i: (bi, 0, 0)),
                  pl.BlockSpec((1, LANES, LANES), lambda bi, i: (bi, 0, 0))],
        out_specs=[pl.BlockSpec((1, tq, NSA_WIDTH), lambda bi, i: (bi, i, 0)),
                   pl.BlockSpec((1, tq, LANES), lambda bi, i: (bi, i, 0))],
        compiler_params=_cparams("parallel", "parallel"),
        name="nsa_cmp_attention",
    )(qk, kc2, vc2)


def _slc_win_kernel(q_ref, ks_ref, vs_ref, kw_ref, vw_ref, sel_ref, gt_ref, oc_ref, ex_ref, o_ref, *, tq, tk):
    i = pl.program_id(1)
    lane = lax.broadcasted_iota(jnp.int32, (1, LANES), 1)
    tpos = i * tq + lax.broadcasted_iota(jnp.int32, (tq, 1), 0)
    kidx = lax.broadcasted_iota(jnp.int32, (1, tk), 1)
    sel = sel_ref[0]
    span = WINDOW + tq
    wstart = pl.multiple_of(jnp.maximum(i * tq - WINDOW, 0), tq)
    wpos = wstart + lax.broadcasted_iota(jnp.int32, (1, span), 1)
    wmask = (wpos <= tpos) & (tpos - wpos < WINDOW)
    o_slc, o_win = [], []
    for g in range(2):
        hm = jnp.where(lane // HEAD_DIM == g, QK_SCALE, 0.0).astype(BF16)
        qg = jnp.concatenate([q_ref[0, :, m * LANES:(m + 1) * LANES] * hm for m in range(4)], axis=0)

        def body(j, state):
            mx, l, acc = state
            r0 = pl.multiple_of(j * tk, tk)
            s = _dot_nt(qg, ks_ref[0, pl.ds(r0, tk), :]).reshape(4, tq, tk)
            allowed = (_dot(sel, ex_ref[g, j]) > 0.5) & ((r0 + kidx) <= tpos)
            s = jnp.where(allowed[None], s, NEG)
            mnew = jnp.maximum(mx, jnp.max(s, axis=2, keepdims=True))
            p = jnp.where(allowed[None], jnp.exp(s - mnew), 0.0)
            a = jnp.exp(mx - mnew)
            l = a * l + jnp.sum(p, axis=2, keepdims=True)
            pv = _dot(p.reshape(4 * tq, tk).astype(BF16), vs_ref[0, pl.ds(r0, tk), :])
            acc = a * acc + pv.reshape(4, tq, LANES)
            return mnew, l, acc

        init = (jnp.full((4, tq, 1), NEG, F32), jnp.zeros((4, tq, 1), F32), jnp.zeros((4, tq, LANES), F32))
        _, l, acc = lax.fori_loop(0, (i + 1) * (tq // tk), body, init)
        o_slc.append(acc / l)

        s = _dot_nt(qg, kw_ref[0, pl.ds(wstart, span), :]).reshape(4, tq, span)
        s = jnp.where(wmask[None], s, NEG)
        e = jnp.where(wmask[None], jnp.exp(s - jnp.max(s, axis=2, keepdims=True)), 0.0)
        p = e / jnp.sum(e, axis=2, keepdims=True)
        o_win.append(_dot(p.reshape(4 * tq, span).astype(BF16),
                          vw_ref[0, pl.ds(wstart, span), :]).reshape(4, tq, LANES))

    gates = _sigmoid(gt_ref[0])
    low = lane < HEAD_DIM
    for m in range(4):
        def gate(br):
            c0, c1 = m * 3 + br, 12 + m * 3 + br
            return jnp.where(low, gates[:, c0:c0 + 1], gates[:, c1:c1 + 1])
        out = (gate(0) * oc_ref[0, :, m * LANES:(m + 1) * LANES]
               + gate(1) * jnp.where(low, o_slc[0][m], o_slc[1][m])
               + gate(2) * jnp.where(low, o_win[0][m], o_win[1][m]))
        o_ref[0, :, m * LANES:(m + 1) * LANES] = out.astype(o_ref.dtype)


def _slc_win_attention(qk, sel, gt, o_cmp, tq=128, tk=128):
    b, s, _ = qk.shape
    assert s >= WINDOW + tq
    nt = s // tk
    l_idx = np.arange(LANES)[None, None, :, None]
    key = (np.arange(nt)[None, :, None, None] * tk + np.arange(tk)[None, None, None, :])
    expand = (l_idx == np.arange(2)[:, None, None, None] * SEL_LANES + key // SEL_BLOCK)
    expand = jnp.asarray(expand, BF16)
    kv = lambda col: pl.BlockSpec((1, s, LANES), lambda bi, i: (bi, 0, col))
    return pl.pallas_call(
        functools.partial(_slc_win_kernel, tq=tq, tk=tk),
        out_shape=jax.ShapeDtypeStruct((b, s, NSA_WIDTH), BF16),
        grid=(b, s // tq),
        in_specs=[pl.BlockSpec((1, tq, NSA_WIDTH), lambda bi, i: (bi, i, 0)),
                  kv(4), kv(5), kv(6), kv(7),
                  pl.BlockSpec((1, tq, LANES), lambda bi, i: (bi, i, 0)),
                  pl.BlockSpec((1, tq, LANES), lambda bi, i: (bi, i, 0)),
                  pl.BlockSpec((1, tq, NSA_WIDTH), lambda bi, i: (bi, i, 0)),
                  pl.BlockSpec((2, nt, LANES, tk), lambda bi, i: (0, 0, 0, 0))],
        out_specs=pl.BlockSpec((1, tq, NSA_WIDTH), lambda bi, i: (bi, i, 0)),
        compiler_params=_cparams("parallel", "arbitrary"),
        name="nsa_slc_win",
    )(qk, qk, qk, qk, qk, sel, gt, o_cmp, expand)


def _gmlp_kernel(u_ref, v_ref, lg_ref, lb_ref, ws_ref, bs_ref, o_ref, *, tc):
    cg = GMLP_WIDTH // GMLP_GROUPS
    u = _gelu_tanh(u_ref[0])
    v = _layer_norm(_gelu_tanh(v_ref[0]), lg_ref[...], lb_ref[...]).astype(BF16)
    rr = lax.broadcasted_iota(jnp.int32, (GMLP_CHUNK, GMLP_CHUNK), 0)
    cc = lax.broadcasted_iota(jnp.int32, (GMLP_CHUNK, GMLP_CHUNK), 1)
    for g in range(GMLP_GROUPS):
        wm = jnp.where(rr >= cc, ws_ref[g], 0.0).astype(BF16)
        bias = bs_ref[:, g:g + 1]
        for c0 in range(0, tc, GMLP_CHUNK):
            mixed = _dot(wm, v[c0:c0 + GMLP_CHUNK, g * cg:(g + 1) * cg]) + bias
            o_ref[0, c0:c0 + GMLP_CHUNK, g * cg:(g + 1) * cg] = (
                u[c0:c0 + GMLP_CHUNK, g * cg:(g + 1) * cg] * mixed).astype(o_ref.dtype)


def _gmlp(uv, ln_g, ln_b, ws, bs, tc=512):
    b, s, _ = uv.shape
    vec = lambda a: a.reshape(1, GMLP_WIDTH).astype(F32)
    return pl.pallas_call(
        functools.partial(_gmlp_kernel, tc=tc),
        out_shape=jax.ShapeDtypeStruct((b, s, GMLP_WIDTH), BF16),
        grid=(b, s // tc),
        in_specs=[pl.BlockSpec((1, tc, GMLP_WIDTH), lambda bi, c: (bi, c, 0)),
                  pl.BlockSpec((1, tc, GMLP_WIDTH), lambda bi, c: (bi, c, 1)),
                  pl.BlockSpec((1, GMLP_WIDTH), lambda bi, c: (0, 0)),
                  pl.BlockSpec((1, GMLP_WIDTH), lambda bi, c: (0, 0)),
                  pl.BlockSpec((GMLP_GROUPS, GMLP_CHUNK, GMLP_CHUNK), lambda bi, c: (0, 0, 0)),
                  pl.BlockSpec((GMLP_CHUNK, GMLP_GROUPS), lambda bi, c: (0, 0))],
        out_specs=pl.BlockSpec((1, tc, GMLP_WIDTH), lambda bi, c: (bi, c, 0)),
        compiler_params=_cparams("parallel", "parallel"),
        name="gmlp",
    )(uv, uv, vec(ln_g), vec(ln_b), ws.astype(F32), jnp.transpose(bs).astype(F32))


def _head_pair_perm():
    m, g, d = np.meshgrid(np.arange(4), np.arange(2), np.arange(HEAD_DIM), indexing="ij")
    return ((g * 4 + m) * HEAD_DIM + d).reshape(-1)


def _even_mixer(x, w_in, conv_w, conv_b, conv_ln_g, conv_ln_b, w_out, ln_g, ln_b, b, s):
    qkv, ag = _proj(x, w_in.astype(BF16), [(0, 3 * SB_WIDTH), (3 * SB_WIDTH, 2 * CONV_CH)], [BF16, F32])
    o_sb = _sb_attention(qkv.reshape(b, s, -1))
    o_cv = _conformer_conv(ag.reshape(b, s, -1), conv_w, conv_b, conv_ln_g, conv_ln_b)
    return _out_proj_ln(o_sb.reshape(b * s, -1), o_cv.reshape(b * s, -1), w_out.astype(BF16), x, ln_g, ln_b)


def _odd_mixer(x, w_in, cmpk_pos, cmpk_w1, cmpk_w2, cmpv_pos, cmpv_w1, cmpv_w2,
               gmlp_ln_g, gmlp_ln_b, gmlp_ws, gmlp_bs, w_out, ln_g, ln_b, b, s):
    perm = _head_pair_perm()
    o_kc, o_ks, o_gt, o_u = NSA_WIDTH, NSA_WIDTH + 2 * KV_WIDTH, NSA_WIDTH + 6 * KV_WIDTH, NSA_WIDTH + 6 * KV_WIDTH + 24
    w_gt = jnp.pad(w_in[:, o_gt:o_u], ((0, 0), (0, LANES - 24)))
    w_re = jnp.concatenate([w_in[:, perm], w_in[:, o_ks:o_gt], w_in[:, o_kc:o_ks], w_in[:, o_u:], w_gt], axis=1)
    qk, cv, uv, gt = _proj(x, w_re.astype(BF16),
                           [(0, 1024), (1024, 256), (1280, 1024), (2304, LANES)], [BF16, F32, F32, F32])
    n16 = s // CMP_STRIDE
    n_cmp = (s - CMP_BLOCK) // CMP_STRIDE + 1
    cv6 = cv.reshape(b, n16, CMP_STRIDE, 2, 2, HEAD_DIM)
    xk = jnp.transpose(cv6[:, :, :, 0], (0, 3, 1, 2, 4)).reshape(b, 2, n16, CMP_STRIDE * HEAD_DIM)
    xv = jnp.transpose(cv6[:, :, :, 1], (0, 3, 1, 2, 4)).reshape(b, 2, n16, CMP_STRIDE * HEAD_DIM)
    kcc, vcc = _compress(xk, xv, cmpk_pos, cmpv_pos, cmpk_w1, cmpv_w1, cmpk_w2, cmpv_w2)
    lanes2 = lambda a: jnp.transpose(a, (0, 2, 1, 3)).reshape(b, n16, 2 * HEAD_DIM).astype(BF16)
    qk3 = qk.reshape(b, s, -1)
    o_cmp, sel = _cmp_attention(qk3, lanes2(kcc), lanes2(vcc), n_cmp)
    o_nsa = _slc_win_attention(qk3, sel, gt.reshape(b, s, -1), o_cmp)
    o_mlp = _gmlp(uv.reshape(b, s, -1), gmlp_ln_g, gmlp_ln_b, gmlp_ws, gmlp_bs)
    w_out_re = jnp.concatenate([w_out[:NSA_WIDTH][perm], w_out[NSA_WIDTH:]], axis=0)
    return _out_proj_ln(o_nsa.reshape(b * s, -1), o_mlp.reshape(b * s, -1), w_out_re.astype(BF16), x, ln_g, ln_b)


def kernel(x, ev_w_in, ev_conv_w, ev_conv_b, ev_conv_ln_g, ev_conv_ln_b, ev_w_out, od_w_in, od_cmpk_pos, od_cmpk_w1, od_cmpk_w2, od_cmpv_pos, od_cmpv_w1, od_cmpv_w2, od_gmlp_ln_g, od_gmlp_ln_b, od_gmlp_ws, od_gmlp_bs, od_w_out, ffn_w_gate, ffn_w_up, ffn_w_down, ln1_g, ln1_b, ln2_g, ln2_b):
    b, s, d = x.shape
    assert s // CMP_STRIDE == LANES and s // SEL_BLOCK == 32
    h = x.reshape(b * s, d)
    for layer in range(DEPTH):
        i = layer // 2
        if layer % 2 == 0:
            h = _even_mixer(h, ev_w_in[i], ev_conv_w[i], ev_conv_b[i], ev_conv_ln_g[i], ev_conv_ln_b[i],
                            ev_w_out[i], ln1_g[layer], ln1_b[layer], b, s)
        else:
            h = _odd_mixer(h, od_w_in[i], od_cmpk_pos[i], od_cmpk_w1[i], od_cmpk_w2[i],
                           od_cmpv_pos[i], od_cmpv_w1[i], od_cmpv_w2[i], od_gmlp_ln_g[i], od_gmlp_ln_b[i],
                           od_gmlp_ws[i], od_gmlp_bs[i], od_w_out[i], ln1_g[layer], ln1_b[layer], b, s)
        h = _ffn_ln(h, ffn_w_gate[layer].astype(BF16), ffn_w_up[layer].astype(BF16),
                    ffn_w_down[layer].astype(BF16), ln2_g[layer], ln2_b[layer])
    return h.reshape(b, s, d)
```

```python
import functools

import numpy as np
import jax
import jax.numpy as jnp
from jax import lax
from jax.experimental import pallas as pl
from jax.experimental.pallas import tpu as pltpu

F32 = jnp.float32
BF16 = jnp.bfloat16

D_MODEL = 1024
DEPTH = 2
HEAD_DIM = 64
LANES = 128
SB_WIDTH = 512
SB_HEADS = 8
CONV_CH = 512
CONV_WIDTH = 31
NSA_WIDTH = 512
KV_WIDTH = 128
CMP_BLOCK = 32
CMP_STRIDE = 16
SEL_BLOCK = 64
SEL_TOPK = 8
SEL_LANES = 32
WINDOW = 512
GMLP_WIDTH = 512
GMLP_GROUPS = 4
GMLP_CHUNK = 128
D_FF = 2816
ALPHA = (2 * DEPTH) ** 0.25
LN_EPS = 1e-5
NEG = -1e30
FORCE_BONUS = 1e3
QK_SCALE = HEAD_DIM ** -0.5

VMEM_LIMIT = 56 * 1024 * 1024


def _cparams(*sem):
    return pltpu.CompilerParams(dimension_semantics=sem, vmem_limit_bytes=VMEM_LIMIT)


def _layer_norm(y, g, b):
    mu = jnp.mean(y, axis=-1, keepdims=True)
    d = y - mu
    var = jnp.mean(d * d, axis=-1, keepdims=True)
    return d * lax.rsqrt(var + LN_EPS) * g + b


def _sigmoid(x):
    return 1.0 / (1.0 + jnp.exp(-x))


def _gelu_tanh(x):
    c = np.float32(np.sqrt(2.0 / np.pi))
    return x * (0.5 * (1.0 + jnp.tanh(c * (x + 0.044715 * (x * x * x)))))


def _dot(a, b):
    return jnp.dot(a, b, preferred_element_type=F32)


def _dot_nt(a, b):
    return lax.dot_general(a, b, (((1,), (1,)), ((), ())), preferred_element_type=F32)


def _proj_kernel(x_ref, w_ref, *out_refs, splits):
    xb = x_ref[...].astype(BF16)
    for (start, width), o_ref in zip(splits, out_refs):
        o_ref[...] = _dot(xb, w_ref[:, start:start + width]).astype(o_ref.dtype)


def _proj(x, w, splits, dtypes, tm=512):
    m, k = x.shape
    n = w.shape[1]
    return pl.pallas_call(
        functools.partial(_proj_kernel, splits=tuple(splits)),
        out_shape=[jax.ShapeDtypeStruct((m, wd), dt) for (_, wd), dt in zip(splits, dtypes)],
        grid=(m // tm,),
        in_specs=[pl.BlockSpec((tm, k), lambda i: (i, 0)),
                  pl.BlockSpec((k, n), lambda i: (0, 0))],
        out_specs=[pl.BlockSpec((tm, wd), lambda i: (i, 0)) for (_, wd) in splits],
        compiler_params=_cparams("parallel"),
        name="proj_in",
    )(x, w)


SB_UNDERFLOW = 104.0


def _sb_kernel(q_ref, k_ref, vt_ref, o_ref, *, t, nh):
    i = pl.program_id(2)
    lane = lax.broadcasted_iota(jnp.int32, (1, LANES), 1)
    rr = lax.broadcasted_iota(jnp.int32, (t, t), 0)
    cc = lax.broadcasted_iota(jnp.int32, (t, t), 1)
    before = rr < cc
    later = jnp.where(before, 1.0, 0.0).astype(BF16)
    hs = range(nh)
    pair = lambda h: slice((h // 2) * LANES, (h // 2 + 1) * LANES)
    qh = [q_ref[0, :, pair(h)] * jnp.where(lane // HEAD_DIM == h % 2, QK_SCALE, 0.0).astype(BF16) for h in hs]

    def tile(j, state, diag):
        r0 = pl.multiple_of(j * t, t)
        z = [_dot_nt(k_ref[0, pl.ds(r0, t), pair(h)], qh[h]) for h in hs]
        lg = [jnp.log(1.0 + jnp.exp(-jnp.abs(z[h]))) for h in hs]
        lr = [-(jnp.maximum(z[h], 0.0) + lg[h]) for h in hs]
        if diag:
            lr = [jnp.where(before, lr[h], 0.0) for h in hs]
        hi = [lr[h].astype(BF16) for h in hs]
        hl = [jnp.concatenate([hi[h], (lr[h] - hi[h].astype(F32)).astype(BF16)], axis=1) for h in hs]
        bt = [_dot(later, hl[h]) for h in hs]
        w = [jnp.exp(jnp.minimum(z[h], 0.0) - lg[h] + (bt[h][:, :t] + bt[h][:, t:] + state[h][0])) for h in hs]
        if diag:
            w = [jnp.where(before, w[h], 0.0) for h in hs]
        pv = [_dot(vt_ref[0, j, h * HEAD_DIM:(h + 1) * HEAD_DIM, :], w[h].astype(BF16)) for h in hs]
        return tuple((state[h][0] + jnp.sum(lr[h], axis=0, keepdims=True), state[h][1] + pv[h]) for h in hs)

    zero = (jnp.zeros((1, t), F32), jnp.zeros((HEAD_DIM, t), F32))
    state = tile(i, (zero,) * nh, True)

    def cond(c):
        return (c[0] >= 0) & (c[1] > -SB_UNDERFLOW)

    def body(c):
        st = tile(c[0], c[2], False)
        top = st[0][0]
        for h in range(1, nh):
            top = jnp.maximum(top, st[h][0])
        return c[0] - 1, jnp.max(top), st

    _, _, state = lax.while_loop(cond, body, (i - 1, jnp.float32(0.0), state))
    for p in range(nh // 2):
        o_ref[0, :, p * LANES:(p + 1) * LANES] = jnp.concatenate(
            [state[2 * p][1], state[2 * p + 1][1]], axis=0).T.astype(o_ref.dtype)


def _sb_attention(qkv, t=256, nh=SB_HEADS):
    b, s, _ = qkv.shape
    wb = nh * HEAD_DIM
    nblk = SB_WIDTH // wb
    vt = jnp.swapaxes(qkv[:, :, 2 * SB_WIDTH:].reshape(b, s // t, t, SB_WIDTH), 2, 3)
    return pl.pallas_call(
        functools.partial(_sb_kernel, t=t, nh=nh),
        out_shape=jax.ShapeDtypeStruct((b, s, SB_WIDTH), BF16),
        grid=(b, nblk, s // t),
        in_specs=[pl.BlockSpec((1, t, wb), lambda bi, hb, i: (bi, i, hb)),
                  pl.BlockSpec((1, s, wb), lambda bi, hb, i: (bi, 0, nblk + hb)),
                  pl.BlockSpec((1, s // t, wb, t), lambda bi, hb, i: (bi, 0, hb, 0))],
        out_specs=pl.BlockSpec((1, t, wb), lambda bi, hb, i: (bi, i, hb)),
        compiler_params=_cparams("parallel", "parallel", "arbitrary"),
        name="sb_attention",
    )(qkv, qkv, vt)


CONV_PAD = 32


def _conv_kernel(a_ref, g_ref, w_ref, b_ref, lg_ref, lb_ref, o_ref, hbuf, *, tc, sub):
    c = pl.program_id(1)

    @pl.when(c == 0)
    def _():
        hbuf[0:CONV_PAD, :] = jnp.zeros((CONV_PAD, CONV_CH), F32)

    hbuf[CONV_PAD:CONV_PAD + tc, :] = a_ref[0] * _sigmoid(g_ref[0])
    off = CONV_PAD - (CONV_WIDTH - 1)
    for r0 in range(0, tc, sub):
        acc = jnp.zeros((sub, CONV_CH), F32) + b_ref[...]
        for w in range(CONV_WIDTH):
            acc = acc + w_ref[w:w + 1, :] * hbuf[r0 + off + w:r0 + off + w + sub, :]
        y = _layer_norm(acc, lg_ref[...], lb_ref[...])
        o_ref[0, r0:r0 + sub, :] = (y * _sigmoid(y)).astype(o_ref.dtype)
    hbuf[0:CONV_PAD, :] = hbuf[tc:tc + CONV_PAD, :]


def _conformer_conv(ag, w_dw, b_dw, ln_g, ln_b, tc=256, sub=64):
    b, s, _ = ag.shape
    vec = lambda v: v.reshape(1, CONV_CH).astype(F32)
    return pl.pallas_call(
        functools.partial(_conv_kernel, tc=tc, sub=sub),
        out_shape=jax.ShapeDtypeStruct((b, s, CONV_CH), BF16),
        grid=(b, s // tc),
        in_specs=[pl.BlockSpec((1, tc, CONV_CH), lambda bi, c: (bi, c, 0)),
                  pl.BlockSpec((1, tc, CONV_CH), lambda bi, c: (bi, c, 1)),
                  pl.BlockSpec((CONV_WIDTH, CONV_CH), lambda bi, c: (0, 0)),
                  pl.BlockSpec((1, CONV_CH), lambda bi, c: (0, 0)),
                  pl.BlockSpec((1, CONV_CH), lambda bi, c: (0, 0)),
                  pl.BlockSpec((1, CONV_CH), lambda bi, c: (0, 0))],
        out_specs=pl.BlockSpec((1, tc, CONV_CH), lambda bi, c: (bi, c, 0)),
        scratch_shapes=[pltpu.VMEM((CONV_PAD + tc, CONV_CH), F32)],
        compiler_params=_cparams("parallel", "arbitrary"),
        name="conformer_conv",
    )(ag, ag, w_dw.astype(F32), vec(b_dw), vec(ln_g), vec(ln_b))


def _out_kernel(l_ref, r_ref, w_ref, x_ref, g_ref, b_ref, o_ref):
    kl = l_ref.shape[1]
    m = _dot(l_ref[...], w_ref[0:kl, :]) + _dot(r_ref[...], w_ref[kl:, :])
    o_ref[...] = _layer_norm(ALPHA * x_ref[...] + m, g_ref[...], b_ref[...])


def _out_proj_ln(left, right, w, x, g, b, tm=512):
    m, d = x.shape
    kl, kr = left.shape[1], right.shape[1]
    return pl.pallas_call(
        _out_kernel,
        out_shape=jax.ShapeDtypeStruct((m, d), F32),
        grid=(m // tm,),
        in_specs=[pl.BlockSpec((tm, kl), lambda i: (i, 0)),
                  pl.BlockSpec((tm, kr), lambda i: (i, 0)),
                  pl.BlockSpec((kl + kr, d), lambda i: (0, 0)),
                  pl.BlockSpec((tm, d), lambda i: (i, 0)),
                  pl.BlockSpec((1, d), lambda i: (0, 0)),
                  pl.BlockSpec((1, d), lambda i: (0, 0))],
        out_specs=pl.BlockSpec((tm, d), lambda i: (i, 0)),
        compiler_params=_cparams("parallel"),
        name="out_proj_ln",
    )(left, right, w, x, g.reshape(1, d).astype(F32), b.reshape(1, d).astype(F32))


FF_CHUNK = 256


def _ffn_kernel(x_ref, wg_ref, wu_ref, wd_ref, g_ref, b_ref, o_ref):
    x = x_ref[...]
    xb = x.astype(BF16)
    acc = jnp.zeros(x.shape, F32)
    for c0 in range(0, D_FF, FF_CHUNK):
        gate = _dot(xb, wg_ref[:, c0:c0 + FF_CHUNK])
        up = _dot(xb, wu_ref[:, c0:c0 + FF_CHUNK])
        h = (gate * _sigmoid(gate) * up).astype(BF16)
        acc = acc + _dot(h, wd_ref[c0:c0 + FF_CHUNK, :])
    o_ref[...] = _layer_norm(ALPHA * x + acc, g_ref[...], b_ref[...])


def _ffn_ln(x, wg, wu, wd, g, b, tm=512):
    m, d = x.shape
    ff = wg.shape[1]
    return pl.pallas_call(
        _ffn_kernel,
        out_shape=jax.ShapeDtypeStruct((m, d), F32),
        grid=(m // tm,),
        in_specs=[pl.BlockSpec((tm, d), lambda i: (i, 0)),
                  pl.BlockSpec((d, ff), lambda i: (0, 0), pipeline_mode=pl.Buffered(1)),
                  pl.BlockSpec((d, ff), lambda i: (0, 0), pipeline_mode=pl.Buffered(1)),
                  pl.BlockSpec((ff, d), lambda i: (0, 0), pipeline_mode=pl.Buffered(1)),
                  pl.BlockSpec((1, d), lambda i: (0, 0)),
                  pl.BlockSpec((1, d), lambda i: (0, 0))],
        out_specs=pl.BlockSpec((tm, d), lambda i: (i, 0)),
        compiler_params=_cparams("parallel"),
        name="ffn_ln",
    )(x, wg, wu, wd, g.reshape(1, d).astype(F32), b.reshape(1, d).astype(F32))


def _compress_kernel(xk_ref, xv_ref, pk_ref, pv_ref, w1k_ref, w1v_ref, w2k_ref, w2v_ref, ok_ref, ov_ref):
    half = (CMP_BLOCK // 2) * HEAD_DIM
    n = xk_ref.shape[2]
    for x_ref, p_ref, w1_ref, w2_ref, o_ref in ((xk_ref, pk_ref, w1k_ref, w2k_ref, ok_ref),
                                                (xv_ref, pv_ref, w1v_ref, w2v_ref, ov_ref)):
        for g in range(2):
            x = x_ref[0, g]
            top = _dot((x + p_ref[0:1, :]).astype(BF16), w1_ref[0:half, :])
            bot = _dot((x + p_ref[1:2, :]).astype(BF16), w1_ref[half:, :])
            h = top + pltpu.roll(bot, n - 1, 0)
            o_ref[0, g] = _dot(_gelu_tanh(h).astype(BF16), w2_ref[...])


def _compress(xk, xv, pos_k, pos_v, w1k, w1v, w2k, w2v):
    b, _, n, half = xk.shape
    xspec = pl.BlockSpec((1, 2, n, half), lambda bi: (bi, 0, 0, 0))
    full = lambda a: pl.BlockSpec(a.shape, lambda bi: (0,) * a.ndim)
    ospec = pl.BlockSpec((1, 2, n, HEAD_DIM), lambda bi: (bi, 0, 0, 0))
    args = (xk, xv, pos_k.reshape(2, half).astype(F32), pos_v.reshape(2, half).astype(F32),
            w1k.astype(BF16), w1v.astype(BF16), w2k.astype(BF16), w2v.astype(BF16))
    return pl.pallas_call(
        _compress_kernel,
        out_shape=[jax.ShapeDtypeStruct((b, 2, n, HEAD_DIM), F32)] * 2,
        grid=(b,),
        in_specs=[xspec, xspec] + [full(a) for a in args[2:]],
        out_specs=[ospec, ospec],
        compiler_params=_cparams("parallel"),
        name="nsa_compress",
    )(*args)


NSA_GROUP = 4


def _nsa_kernel(q_ref, ks_ref, vst_ref, kw_ref, vwt_ref, kc_ref, vct_ref, gt_ref, o_ref, selb_ref, *, t, n_cmp):
    i = pl.program_id(1)
    nh = NSA_GROUP
    lane = lax.broadcasted_iota(jnp.int32, (1, LANES), 1)
    tpos = i * t + lax.broadcasted_iota(jnp.int32, (1, t), 1)
    krow = lax.broadcasted_iota(jnp.int32, (t, 1), 0)
    qg = []
    for g in range(2):
        hm = jnp.where(lane // HEAD_DIM == g, QK_SCALE, 0.0).astype(BF16)
        qg.append(jnp.concatenate([q_ref[0, :, m * LANES:(m + 1) * LANES] * hm for m in range(nh)], axis=0))

    nblk = lax.broadcasted_iota(jnp.int32, (LANES, 1), 0)
    cvalid = (nblk * CMP_STRIDE + (CMP_BLOCK - 1) <= tpos) & (nblk < n_cmp)
    jr = lax.broadcasted_iota(jnp.int32, (SEL_LANES, LANES), 0)
    nc = lax.broadcasted_iota(jnp.int32, (SEL_LANES, LANES), 1)
    ovl_t = jnp.where((nc * CMP_STRIDE < jr * SEL_BLOCK + SEL_BLOCK) & (nc * CMP_STRIDE + CMP_BLOCK > jr * SEL_BLOCK)
                      & (nc < n_cmp), 1.0, 0.0).astype(BF16)
    jsel = lax.broadcasted_iota(jnp.int32, (SEL_LANES, 1), 0)
    blk_t = tpos // SEL_BLOCK
    forced = (jsel == 0) | (jsel == blk_t) | (jsel == blk_t - 1)
    o_cmp = []
    for g in range(2):
        st = _dot_nt(kc_ref[0], qg[g])
        ps = []
        psum = jnp.zeros((LANES, t), F32)
        for m in range(nh):
            s = jnp.where(cvalid, st[:, m * t:(m + 1) * t], NEG)
            e = jnp.where(cvalid, jnp.exp(s - jnp.max(s, axis=0, keepdims=True)), 0.0)
            l = jnp.sum(e, axis=0, keepdims=True)
            p = e * (1.0 / jnp.where(l > 0.0, l, 1.0))
            ps.append(p.astype(BF16))
            psum = psum + p
        o_cmp.append(_dot(vct_ref[0, g * HEAD_DIM:(g + 1) * HEAD_DIM, :], jnp.concatenate(ps, axis=1)))
        hi = psum.astype(BF16)
        lo = (psum - hi.astype(F32)).astype(BF16)
        it = _dot(ovl_t, jnp.concatenate([hi, lo], axis=1))
        imp = it[:, :t] + it[:, t:]
        score = jnp.where(jsel <= blk_t, imp + jnp.where(forced, FORCE_BONUS, 0.0), NEG)
        cnt = jnp.zeros((SEL_LANES, t), F32)
        for c in range(SEL_LANES):
            row = score[c:c + 1, :]
            cnt = cnt + jnp.where((row > score) | ((row == score) & (jsel > c)), 1.0, 0.0)
        selb = jnp.where(cnt < SEL_TOPK, 0.0, NEG)
        for c in range(SEL_LANES):
            selb_ref[g, c] = selb[c:c + 1, :]

    def attend(streams):
        sb = [_dot_nt(k, q) + jnp.concatenate([bias] * nh, axis=1) for k, _, q, bias, _ in streams]
        mnew = [jnp.maximum(st[0], jnp.max(s, axis=0, keepdims=True)) for s, (_, _, _, _, st) in zip(sb, streams)]
        p = [jnp.exp(s - mn) for s, mn in zip(sb, mnew)]
        a = [jnp.exp(st[0] - mn) for mn, (_, _, _, _, st) in zip(mnew, streams)]
        pv = [_dot(vt, pp.astype(BF16)) for pp, (_, vt, _, _, _) in zip(p, streams)]
        return [(mn, aa * st[1] + jnp.sum(pp, axis=0, keepdims=True), aa * st[2] + o)
                for mn, aa, pp, o, (_, _, _, _, st) in zip(mnew, a, p, pv, streams)]

    init = (jnp.full((1, nh * t), NEG, F32), jnp.zeros((1, nh * t), F32), jnp.zeros((HEAD_DIM, nh * t), F32))
    per_tile = t // SEL_BLOCK
    rows_g = lambda g: slice(g * HEAD_DIM, (g + 1) * HEAD_DIM)

    def slc_streams(j, jc, state):
        kpos = j * t + krow
        causal = jnp.where((kpos >= 0) & (kpos <= tpos), 0.0, NEG)
        r0 = pl.multiple_of(jc * t, t)
        out = []
        for g in range(2):
            rows = [jnp.broadcast_to(selb_ref[g, jc * per_tile + b], (SEL_BLOCK, t)) for b in range(per_tile)]
            out.append((ks_ref[0, pl.ds(r0, t), :], vst_ref[0, jc, rows_g(g), :], qg[g],
                        jnp.concatenate(rows, axis=0) + causal, state[g]))
        return out

    slc = [init, init]
    win = [init, init]
    n_win = WINDOW // t + 1
    for d in range(n_win):
        j = i - d
        jc = jnp.maximum(j, 0)
        kpos = j * t + krow
        wbias = jnp.where((kpos >= 0) & (kpos <= tpos) & (tpos - kpos < WINDOW), 0.0, NEG)
        r0 = pl.multiple_of(jc * t, t)
        wstreams = [(kw_ref[0, pl.ds(r0, t), :], vwt_ref[0, jc, rows_g(g), :], qg[g], wbias, win[g]) for g in range(2)]
        res = attend(slc_streams(j, jc, slc) + wstreams)
        slc, win = res[:2], res[2:]

    def slc_body(it, state):
        j = i - it
        return tuple(attend(slc_streams(j, j, state)))

    slc = lax.fori_loop(n_win, i + 1, slc_body, tuple(slc))

    gates = _sigmoid(gt_ref[0]).T
    for m in range(nh):
        halves = []
        for g in range(2):
            c = g * 3 * nh + m * 3
            sl = slice(m * t, (m + 1) * t)
            o_s = slc[g][2][:, sl] * (1.0 / slc[g][1][:, sl])
            o_w = win[g][2][:, sl] * (1.0 / win[g][1][:, sl])
            halves.append(gates[c:c + 1, :] * o_cmp[g][:, sl] + gates[c + 1:c + 2, :] * o_s + gates[c + 2:c + 3, :] * o_w)
        o_ref[0, :, m * LANES:(m + 1) * LANES] = jnp.concatenate(halves, axis=0).T.astype(o_ref.dtype)


def _nsa_attention(qk, kc2, vct, gt, n_cmp, t=256):
    b, s, _ = qk.shape
    assert WINDOW % t == 0 and t % SEL_BLOCK == 0
    nt = s // t
    tiles_t = lambda c0: jnp.swapaxes(qk[:, :, c0:c0 + LANES].reshape(b, nt, t, LANES), 2, 3)
    kspec = lambda col: pl.BlockSpec((1, s, LANES), lambda bi, i: (bi, 0, col))
    vspec = pl.BlockSpec((1, nt, LANES, t), lambda bi, i: (bi, 0, 0, 0))
    cspec = pl.BlockSpec((1, LANES, LANES), lambda bi, i: (bi, 0, 0))
    return pl.pallas_call(
        functools.partial(_nsa_kernel, t=t, n_cmp=n_cmp),
        out_shape=jax.ShapeDtypeStruct((b, s, NSA_WIDTH), BF16),
        grid=(b, nt),
        in_specs=[pl.BlockSpec((1, t, NSA_WIDTH), lambda bi, i: (bi, i, 0)),
                  kspec(4), vspec, kspec(6), vspec, cspec, cspec,
                  pl.BlockSpec((1, t, LANES), lambda bi, i: (bi, i, 0))],
        out_specs=pl.BlockSpec((1, t, NSA_WIDTH), lambda bi, i: (bi, i, 0)),
        scratch_shapes=[pltpu.VMEM((2, SEL_LANES, 1, t), F32)],
        compiler_params=_cparams("parallel", "arbitrary"),
        name="nsa_attention",
    )(qk, qk, tiles_t(5 * LANES), qk, tiles_t(7 * LANES), kc2, vct, gt)


def _gmlp_kernel(u_ref, v_ref, lg_ref, lb_ref, ws_ref, bs_ref, o_ref, *, tc):
    cg = GMLP_WIDTH // GMLP_GROUPS
    u = _gelu_tanh(u_ref[0])
    v = _layer_norm(_gelu_tanh(v_ref[0]), lg_ref[...], lb_ref[...]).astype(BF16)
    rr = lax.broadcasted_iota(jnp.int32, (GMLP_CHUNK, GMLP_CHUNK), 0)
    cc = lax.broadcasted_iota(jnp.int32, (GMLP_CHUNK, GMLP_CHUNK), 1)
    for g in range(GMLP_GROUPS):
        wm = jnp.where(rr >= cc, ws_ref[g], 0.0).astype(BF16)
        bias = bs_ref[:, g:g + 1]
        for c0 in range(0, tc, GMLP_CHUNK):
            mixed = _dot(wm, v[c0:c0 + GMLP_CHUNK, g * cg:(g + 1) * cg]) + bias
            o_ref[0, c0:c0 + GMLP_CHUNK, g * cg:(g + 1) * cg] = (
                u[c0:c0 + GMLP_CHUNK, g * cg:(g + 1) * cg] * mixed).astype(o_ref.dtype)


def _gmlp(uv, ln_g, ln_b, ws, bs, tc=512):
    b, s, _ = uv.shape
    vec = lambda a: a.reshape(1, GMLP_WIDTH).astype(F32)
    return pl.pallas_call(
        functools.partial(_gmlp_kernel, tc=tc),
        out_shape=jax.ShapeDtypeStruct((b, s, GMLP_WIDTH), BF16),
        grid=(b, s // tc),
        in_specs=[pl.BlockSpec((1, tc, GMLP_WIDTH), lambda bi, c: (bi, c, 0)),
                  pl.BlockSpec((1, tc, GMLP_WIDTH), lambda bi, c: (bi, c, 1)),
                  pl.BlockSpec((1, GMLP_WIDTH), lambda bi, c: (0, 0)),
                  pl.BlockSpec((1, GMLP_WIDTH), lambda bi, c: (0, 0)),
                  pl.BlockSpec((GMLP_GROUPS, GMLP_CHUNK, GMLP_CHUNK), lambda bi, c: (0, 0, 0)),
                  pl.BlockSpec((GMLP_CHUNK, GMLP_GROUPS), lambda bi, c: (0, 0))],
        out_specs=pl.BlockSpec((1, tc, GMLP_WIDTH), lambda bi, c: (bi, c, 0)),
        compiler_params=_cparams("parallel", "parallel"),
        name="gmlp",
    )(uv, uv, vec(ln_g), vec(ln_b), ws.astype(F32), jnp.transpose(bs).astype(F32))


def _head_pair_perm():
    m, g, d = np.meshgrid(np.arange(4), np.arange(2), np.arange(HEAD_DIM), indexing="ij")
    return ((g * 4 + m) * HEAD_DIM + d).reshape(-1)


def _even_mixer(x, w_in, conv_w, conv_b, conv_ln_g, conv_ln_b, w_out, ln_g, ln_b, b, s):
    qkv, ag = _proj(x, w_in.astype(BF16), [(0, 3 * SB_WIDTH), (3 * SB_WIDTH, 2 * CONV_CH)], [BF16, F32])
    o_sb = _sb_attention(qkv.reshape(b, s, -1))
    o_cv = _conformer_conv(ag.reshape(b, s, -1), conv_w, conv_b, conv_ln_g, conv_ln_b)
    return _out_proj_ln(o_sb.reshape(b * s, -1), o_cv.reshape(b * s, -1), w_out.astype(BF16), x, ln_g, ln_b)


def _odd_mixer(x, w_in, cmpk_pos, cmpk_w1, cmpk_w2, cmpv_pos, cmpv_w1, cmpv_w2,
               gmlp_ln_g, gmlp_ln_b, gmlp_ws, gmlp_bs, w_out, ln_g, ln_b, b, s):
    perm = _head_pair_perm()
    o_kc, o_ks, o_gt, o_u = NSA_WIDTH, NSA_WIDTH + 2 * KV_WIDTH, NSA_WIDTH + 6 * KV_WIDTH, NSA_WIDTH + 6 * KV_WIDTH + 24
    w_gt = jnp.pad(w_in[:, o_gt:o_u], ((0, 0), (0, LANES - 24)))
    w_re = jnp.concatenate([w_in[:, perm], w_in[:, o_ks:o_gt], w_in[:, o_kc:o_ks], w_in[:, o_u:], w_gt], axis=1)
    qk, cv, uv, gt = _proj(x, w_re.astype(BF16),
                           [(0, 1024), (1024, 256), (1280, 1024), (2304, LANES)], [BF16, F32, F32, F32])
    n16 = s // CMP_STRIDE
    n_cmp = (s - CMP_BLOCK) // CMP_STRIDE + 1
    cv6 = cv.reshape(b, n16, CMP_STRIDE, 2, 2, HEAD_DIM)
    xk = jnp.transpose(cv6[:, :, :, 0], (0, 3, 1, 2, 4)).reshape(b, 2, n16, CMP_STRIDE * HEAD_DIM)
    xv = jnp.transpose(cv6[:, :, :, 1], (0, 3, 1, 2, 4)).reshape(b, 2, n16, CMP_STRIDE * HEAD_DIM)
    kcc, vcc = _compress(xk, xv, cmpk_pos, cmpv_pos, cmpk_w1, cmpv_w1, cmpk_w2, cmpv_w2)
    kc2 = jnp.transpose(kcc, (0, 2, 1, 3)).reshape(b, n16, 2 * HEAD_DIM).astype(BF16)
    vct = jnp.transpose(vcc, (0, 1, 3, 2)).reshape(b, 2 * HEAD_DIM, n16).astype(BF16)
    o_nsa = _nsa_attention(qk.reshape(b, s, -1), kc2, vct, gt.reshape(b, s, -1), n_cmp)
    o_mlp = _gmlp(uv.reshape(b, s, -1), gmlp_ln_g, gmlp_ln_b, gmlp_ws, gmlp_bs)
    w_out_re = jnp.concatenate([w_out[:NSA_WIDTH][perm], w_out[NSA_WIDTH:]], axis=0)
    return _out_proj_ln(o_nsa.reshape(b * s, -1), o_mlp.reshape(b * s, -1), w_out_re.astype(BF16), x, ln_g, ln_b)


def kernel(x, ev_w_in, ev_conv_w, ev_conv_b, ev_conv_ln_g, ev_conv_ln_b, ev_w_out, od_w_in, od_cmpk_pos, od_cmpk_w1, od_cmpk_w2, od_cmpv_pos, od_cmpv_w1, od_cmpv_w2, od_gmlp_ln_g, od_gmlp_ln_b, od_gmlp_ws, od_gmlp_bs, od_w_out, ffn_w_gate, ffn_w_up, ffn_w_down, ln1_g, ln1_b, ln2_g, ln2_b):
    b, s, d = x.shape
    assert s // CMP_STRIDE == LANES and s // SEL_BLOCK == SEL_LANES
    h = x.reshape(b * s, d)
    for layer in range(DEPTH):
        i = layer // 2
        if layer % 2 == 0:
            h = _even_mixer(h, ev_w_in[i], ev_conv_w[i], ev_conv_b[i], ev_conv_ln_g[i], ev_conv_ln_b[i],
                            ev_w_out[i], ln1_g[layer], ln1_b[layer], b, s)
        else:
            h = _odd_mixer(h, od_w_in[i], od_cmpk_pos[i], od_cmpk_w1[i], od_cmpk_w2[i],
                           od_cmpv_pos[i], od_cmpv_w1[i], od_cmpv_w2[i], od_gmlp_ln_g[i], od_gmlp_ln_b[i],
                           od_gmlp_ws[i], od_gmlp_bs[i], od_w_out[i], ln1_g[layer], ln1_b[layer], b, s)
        h = _ffn_ln(h, ffn_w_gate[layer].astype(BF16), ffn_w_up[layer].astype(BF16),
                    ffn_w_down[layer].astype(BF16), ln2_g[layer], ln2_b[layer])
    return h.reshape(b, s, d)
```

```python
import functools

import numpy as np
import jax
import jax.numpy as jnp
from jax import lax
from jax.experimental import pallas as pl
from jax.experimental.pallas import tpu as pltpu

F32 = jnp.float32
BF16 = jnp.bfloat16

D_MODEL = 1024
DEPTH = 2
HEAD_DIM = 64
LANES = 128
SUBLANES = 8
SB_WIDTH = 512
SB_HEADS = 8
CONV_CH = 512
CONV_WIDTH = 31
NSA_WIDTH = 512
KV_WIDTH = 128
CMP_BLOCK = 32
CMP_STRIDE = 16
SEL_BLOCK = 64
SEL_TOPK = 8
SEL_LANES = 32
WINDOW = 512
GMLP_WIDTH = 512
GMLP_GROUPS = 4
GMLP_CHUNK = 128
D_FF = 2816
ALPHA = (2 * DEPTH) ** 0.25
LN_EPS = 1e-5
NEG = -1e30
FORCE_BONUS = 1e3
QK_SCALE = HEAD_DIM ** -0.5
LOG2E = float(np.log2(np.e))
NSA_BIAS_LANE = SEL_LANES

VMEM_LIMIT = 56 * 1024 * 1024


def _cparams(*sem):
    return pltpu.CompilerParams(dimension_semantics=sem, vmem_limit_bytes=VMEM_LIMIT)


def _layer_norm(y, g, b):
    mu = jnp.mean(y, axis=-1, keepdims=True)
    d = y - mu
    var = jnp.mean(d * d, axis=-1, keepdims=True)
    return d * lax.rsqrt(var + LN_EPS) * g + b


def _sigmoid(x):
    return 1.0 / (1.0 + jnp.exp(-x))


def _gelu_tanh(x):
    c = np.float32(np.sqrt(2.0 / np.pi))
    return x * (0.5 * (1.0 + jnp.tanh(c * (x + 0.044715 * (x * x * x)))))


def _dot(a, b):
    return jnp.dot(a, b, preferred_element_type=F32)


def _dot_nt(a, b):
    return lax.dot_general(a, b, (((1,), (1,)), ((), ())), preferred_element_type=F32)


def _proj_kernel(x_ref, w_ref, *out_refs, splits):
    xb = x_ref[...].astype(BF16)
    for (start, width), o_ref in zip(splits, out_refs):
        o_ref[...] = _dot(xb, w_ref[:, start:start + width]).astype(o_ref.dtype)


def _proj(x, w, splits, dtypes, tm=512):
    m, k = x.shape
    n = w.shape[1]
    return pl.pallas_call(
        functools.partial(_proj_kernel, splits=tuple(splits)),
        out_shape=[jax.ShapeDtypeStruct((m, wd), dt) for (_, wd), dt in zip(splits, dtypes)],
        grid=(m // tm,),
        in_specs=[pl.BlockSpec((tm, k), lambda i: (i, 0)),
                  pl.BlockSpec((k, n), lambda i: (0, 0))],
        out_specs=[pl.BlockSpec((tm, wd), lambda i: (i, 0)) for (_, wd) in splits],
        compiler_params=_cparams("parallel"),
        name="proj_in",
    )(x, w)


SB_UNDERFLOW = 150.0


def _sb_kernel(q_ref, k_ref, vt_ref, o_ref, *, t, nh):
    i = pl.program_id(2)
    lane = lax.broadcasted_iota(jnp.int32, (1, LANES), 1)
    rr = lax.broadcasted_iota(jnp.int32, (t, t), 0)
    cc = lax.broadcasted_iota(jnp.int32, (t, t), 1)
    before = rr < cc
    later = jnp.where(before, 1.0, 0.0).astype(BF16)
    hs = range(nh)
    pair = lambda h: slice((h // 2) * LANES, (h // 2 + 1) * LANES)
    qh = [q_ref[0, :, pair(h)] * jnp.where(lane // HEAD_DIM == h % 2, 1.0, 0.0).astype(BF16) for h in hs]

    def tile(j, state, diag):
        r0 = pl.multiple_of(j * t, t)
        z = [_dot_nt(k_ref[0, pl.ds(r0, t), pair(h)], qh[h]) for h in hs]
        lg = [jnp.log2(1.0 + jnp.exp2(-jnp.abs(z[h]))) for h in hs]
        lr = [jnp.minimum(-z[h], 0.0) - lg[h] for h in hs]
        if diag:
            lr = [jnp.where(before, lr[h], 0.0) for h in hs]
        hi = [lr[h].astype(BF16) for h in hs]
        hl = [jnp.concatenate([hi[h], (lr[h] - hi[h].astype(F32)).astype(BF16)], axis=1) for h in hs]
        bt = [_dot(later, hl[h]) for h in hs]
        w = [jnp.exp2(jnp.minimum(z[h], 0.0) - lg[h] + (bt[h][:, :t] + bt[h][:, t:] + state[h][0])) for h in hs]
        if diag:
            w = [jnp.where(before, w[h], 0.0) for h in hs]
        pv = [_dot(vt_ref[0, j, h * HEAD_DIM:(h + 1) * HEAD_DIM, :], w[h].astype(BF16)) for h in hs]
        return tuple((state[h][0] + jnp.sum(lr[h], axis=0, keepdims=True), state[h][1] + pv[h]) for h in hs)

    zero = (jnp.zeros((1, t), F32), jnp.zeros((HEAD_DIM, t), F32))
    state = tile(i, (zero,) * nh, True)

    def cond(c):
        return (c[0] >= 0) & (c[1] > -SB_UNDERFLOW)

    def body(c):
        st = tile(c[0], c[2], False)
        top = st[0][0]
        for h in range(1, nh):
            top = jnp.maximum(top, st[h][0])
        return c[0] - 1, jnp.max(top), st

    _, _, state = lax.while_loop(cond, body, (i - 1, jnp.float32(0.0), state))
    for p in range(nh // 2):
        o_ref[0, :, p * LANES:(p + 1) * LANES] = jnp.concatenate(
            [state[2 * p][1], state[2 * p + 1][1]], axis=0).T.astype(o_ref.dtype)


def _sb_attention(qkv, t=256, nh=SB_HEADS):
    b, s, _ = qkv.shape
    wb = nh * HEAD_DIM
    nblk = SB_WIDTH // wb
    vt = jnp.swapaxes(qkv[:, :, 2 * SB_WIDTH:].reshape(b, s // t, t, SB_WIDTH), 2, 3)
    return pl.pallas_call(
        functools.partial(_sb_kernel, t=t, nh=nh),
        out_shape=jax.ShapeDtypeStruct((b, s, SB_WIDTH), BF16),
        grid=(b, nblk, s // t),
        in_specs=[pl.BlockSpec((1, t, wb), lambda bi, hb, i: (bi, i, hb)),
                  pl.BlockSpec((1, s, wb), lambda bi, hb, i: (bi, 0, nblk + hb)),
                  pl.BlockSpec((1, s // t, wb, t), lambda bi, hb, i: (bi, 0, hb, 0))],
        out_specs=pl.BlockSpec((1, t, wb), lambda bi, hb, i: (bi, i, hb)),
        compiler_params=_cparams("parallel", "parallel", "arbitrary"),
        name="sb_attention",
    )(qkv, qkv, vt)


CONV_PAD = 32


def _conv_kernel(a_ref, g_ref, w_ref, b_ref, lg_ref, lb_ref, o_ref, hbuf, *, tc, sub):
    c = pl.program_id(1)

    @pl.when(c == 0)
    def _():
        hbuf[0:CONV_PAD, :] = jnp.zeros((CONV_PAD, CONV_CH), F32)

    hbuf[CONV_PAD:CONV_PAD + tc, :] = a_ref[0] * _sigmoid(g_ref[0])
    off = CONV_PAD - (CONV_WIDTH - 1)
    for r0 in range(0, tc, sub):
        acc = jnp.zeros((sub, CONV_CH), F32) + b_ref[...]
        for ph in range(SUBLANES):
            taps = [w for w in range(CONV_WIDTH) if (off + w) % SUBLANES == ph]
            rows = sub + (SUBLANES if ph else 0)
            part = None
            for w in taps:
                a0 = r0 + (off + w) // SUBLANES * SUBLANES
                term = w_ref[w:w + 1, :] * hbuf[a0:a0 + rows, :]
                part = term if part is None else part + term
            acc = acc + part[ph:ph + sub, :]
        y = _layer_norm(acc, lg_ref[...], lb_ref[...])
        o_ref[0, r0:r0 + sub, :] = (y * _sigmoid(y)).astype(o_ref.dtype)
    hbuf[0:CONV_PAD, :] = hbuf[tc:tc + CONV_PAD, :]


def _conformer_conv(ag, w_dw, b_dw, ln_g, ln_b, tc=256, sub=64):
    b, s, _ = ag.shape
    vec = lambda v: v.reshape(1, CONV_CH).astype(F32)
    return pl.pallas_call(
        functools.partial(_conv_kernel, tc=tc, sub=sub),
        out_shape=jax.ShapeDtypeStruct((b, s, CONV_CH), BF16),
        grid=(b, s // tc),
        in_specs=[pl.BlockSpec((1, tc, CONV_CH), lambda bi, c: (bi, c, 0)),
                  pl.BlockSpec((1, tc, CONV_CH), lambda bi, c: (bi, c, 1)),
                  pl.BlockSpec((CONV_WIDTH, CONV_CH), lambda bi, c: (0, 0)),
                  pl.BlockSpec((1, CONV_CH), lambda bi, c: (0, 0)),
                  pl.BlockSpec((1, CONV_CH), lambda bi, c: (0, 0)),
                  pl.BlockSpec((1, CONV_CH), lambda bi, c: (0, 0))],
        out_specs=pl.BlockSpec((1, tc, CONV_CH), lambda bi, c: (bi, c, 0)),
        scratch_shapes=[pltpu.VMEM((CONV_PAD + tc, CONV_CH), F32)],
        compiler_params=_cparams("parallel", "arbitrary"),
        name="conformer_conv",
    )(ag, ag, w_dw.astype(F32), vec(b_dw), vec(ln_g), vec(ln_b))


FF_CHUNK = 256


def _post_mixer_kernel(l_ref, r_ref, wo_ref, x_ref, g1_ref, b1_ref, wg_ref, wu_ref, wd_ref, g2_ref, b2_ref, o_ref):
    kl = l_ref.shape[1]
    m = _dot(l_ref[...], wo_ref[0:kl, :]) + _dot(r_ref[...], wo_ref[kl:, :])
    x = _layer_norm(ALPHA * x_ref[...] + m, g1_ref[...], b1_ref[...])
    xb = x.astype(BF16)
    acc = jnp.zeros(x.shape, F32)
    for c0 in range(0, D_FF, FF_CHUNK):
        gate = _dot(xb, wg_ref[:, c0:c0 + FF_CHUNK])
        up = _dot(xb, wu_ref[:, c0:c0 + FF_CHUNK])
        h = (gate * _sigmoid(gate) * up).astype(BF16)
        acc = acc + _dot(h, wd_ref[c0:c0 + FF_CHUNK, :])
    o_ref[...] = _layer_norm(ALPHA * x + acc, g2_ref[...], b2_ref[...])


def _post_mixer(left, right, wo, x, g1, b1, wg, wu, wd, g2, b2, tm=512):
    m, d = x.shape
    kl, kr = left.shape[1], right.shape[1]
    ff = wg.shape[1]
    row = lambda w: pl.BlockSpec((tm, w), lambda i: (i, 0))
    res = lambda shape: pl.BlockSpec(shape, lambda i: (0, 0), pipeline_mode=pl.Buffered(1))
    vec = lambda v: v.reshape(1, d).astype(F32)
    return pl.pallas_call(
        _post_mixer_kernel,
        out_shape=jax.ShapeDtypeStruct((m, d), F32),
        grid=(m // tm,),
        in_specs=[row(kl), row(kr), res((kl + kr, d)), row(d), res((1, d)), res((1, d)),
                  res((d, ff)), res((d, ff)), res((ff, d)), res((1, d)), res((1, d))],
        out_specs=row(d),
        compiler_params=_cparams("parallel"),
        name="post_mixer",
    )(left, right, wo, x, vec(g1), vec(b1), wg, wu, wd, vec(g2), vec(b2))


def _compress_kernel(xk_ref, xv_ref, pos_ref, w1_ref, w2_ref, kc_ref, vct_ref):
    n = xk_ref.shape[1] // CMP_STRIDE
    out = []
    for s, x_ref in enumerate((xk_ref, xv_ref)):
        xs = [x_ref[0, pl.ds(r, n, stride=CMP_STRIDE), :] for r in range(CMP_STRIDE)]
        half = lambda o: jnp.concatenate([x + pos_ref[s, o + r:o + r + 1, :] for r, x in enumerate(xs)], axis=1)
        top = _dot(half(0).astype(BF16), w1_ref[s, 0])
        bot = _dot(half(CMP_STRIDE).astype(BF16), w1_ref[s, 1])
        h = top + pltpu.roll(bot, n - 1, 0)
        out.append(_dot(_gelu_tanh(h).astype(BF16), w2_ref[s]))
    kc_ref[0] = out[0].astype(kc_ref.dtype)
    vct_ref[0] = out[1].astype(vct_ref.dtype).astype(F32).T.astype(vct_ref.dtype)


def _block_diag2(a):
    z = jnp.zeros_like(a)
    return jnp.concatenate([jnp.concatenate([a, z], axis=-1), jnp.concatenate([z, a], axis=-1)], axis=-2)


def _compress(cv, pos_k, pos_v, w1k, w1v, w2k, w2v):
    b, s, _ = cv.shape
    n = s // CMP_STRIDE
    pos = jnp.stack([jnp.concatenate([p, p], axis=-1) for p in (pos_k, pos_v)]).astype(F32)
    w1 = jnp.stack([_block_diag2(w.reshape(CMP_BLOCK, HEAD_DIM, HEAD_DIM)) for w in (w1k, w1v)])
    w1 = w1.reshape(2, 2, CMP_STRIDE * KV_WIDTH, KV_WIDTH).astype(BF16)
    w2 = jnp.stack([_block_diag2(w) for w in (w2k, w2v)]).astype(BF16)
    full = lambda a: pl.BlockSpec(a.shape, lambda bi: (0,) * a.ndim)
    return pl.pallas_call(
        _compress_kernel,
        out_shape=[jax.ShapeDtypeStruct((b, n, KV_WIDTH), BF16), jax.ShapeDtypeStruct((b, KV_WIDTH, n), BF16)],
        grid=(b,),
        in_specs=[pl.BlockSpec((1, s, KV_WIDTH), lambda bi: (bi, 0, 0)),
                  pl.BlockSpec((1, s, KV_WIDTH), lambda bi: (bi, 0, 1)), full(pos), full(w1), full(w2)],
        out_specs=[pl.BlockSpec((1, n, KV_WIDTH), lambda bi: (bi, 0, 0)),
                   pl.BlockSpec((1, KV_WIDTH, n), lambda bi: (bi, 0, 0))],
        compiler_params=_cparams("parallel"),
        name="nsa_compress",
    )(cv, cv, pos, w1, w2)


NSA_GROUP = 4


def _nsa_kernel(q_ref, ks_ref, vst_ref, kw_ref, vwt_ref, kc_ref, vct_ref, gt_ref, o_ref, *, t, n_cmp):
    i = pl.program_id(1)
    nh = NSA_GROUP
    lane = lax.broadcasted_iota(jnp.int32, (1, LANES), 1)
    tpos = i * t + lax.broadcasted_iota(jnp.int32, (1, t), 1)
    krow = lax.broadcasted_iota(jnp.int32, (t, 1), 0)
    qg = []
    for g in range(2):
        hm = jnp.where(lane // HEAD_DIM == g, 1.0, 0.0).astype(BF16)
        qg.append(jnp.concatenate([q_ref[0, :, m * LANES:(m + 1) * LANES] * hm for m in range(nh)], axis=0))

    nblk = lax.broadcasted_iota(jnp.int32, (LANES, 1), 0)
    cvalid = (nblk * CMP_STRIDE + (CMP_BLOCK - 1) <= tpos) & (nblk < n_cmp)
    jr = lax.broadcasted_iota(jnp.int32, (SEL_LANES, LANES), 0)
    nc = lax.broadcasted_iota(jnp.int32, (SEL_LANES, LANES), 1)
    ovl_t = jnp.where((nc * CMP_STRIDE < jr * SEL_BLOCK + SEL_BLOCK) & (nc * CMP_STRIDE + CMP_BLOCK > jr * SEL_BLOCK)
                      & (nc < n_cmp), 1.0, 0.0).astype(BF16)
    jsel = lax.broadcasted_iota(jnp.int32, (SEL_LANES, 1), 0)
    blk_t = tpos // SEL_BLOCK
    forced = (jsel == 0) | (jsel == blk_t) | (jsel == blk_t - 1)
    pad_rows = lax.broadcasted_iota(jnp.int32, (LANES - SEL_LANES, 1), 0)
    o_cmp = []
    qx = []
    for g in range(2):
        st = _dot_nt(kc_ref[0], qg[g])
        ps = []
        psum = jnp.zeros((LANES, t), F32)
        for m in range(nh):
            s = jnp.where(cvalid, st[:, m * t:(m + 1) * t], NEG)
            e = jnp.where(cvalid, jnp.exp2(s - jnp.max(s, axis=0, keepdims=True)), 0.0)
            l = jnp.sum(e, axis=0, keepdims=True)
            p = e * (1.0 / jnp.where(l > 0.0, l, 1.0))
            ps.append(p.astype(BF16))
            psum = psum + p
        o_cmp.append(_dot(vct_ref[0, g * HEAD_DIM:(g + 1) * HEAD_DIM, :], jnp.concatenate(ps, axis=1)))
        hi = psum.astype(BF16)
        lo = (psum - hi.astype(F32)).astype(BF16)
        it = _dot(ovl_t, jnp.concatenate([hi, lo], axis=1))
        imp = it[:, :t] + it[:, t:]
        score = jnp.where(jsel <= blk_t, imp + jnp.where(forced, FORCE_BONUS, 0.0), NEG)
        cnt = jnp.zeros((SEL_LANES, t), F32)
        for c in range(SEL_LANES):
            row = score[c:c + 1, :]
            cnt = cnt + jnp.where((row > score) | ((row == score) & (jsel > c)), 1.0, 0.0)
        tail = jnp.where(pad_rows == NSA_BIAS_LANE - SEL_LANES, NEG, 0.0) + jnp.zeros((1, t), F32)
        selt = jnp.concatenate([jnp.where(cnt < SEL_TOPK, 0.0, NEG), tail], axis=0).T.astype(BF16)
        qx.append(jnp.concatenate([qg[g], jnp.concatenate([selt] * nh, axis=0)], axis=1))

    def attend(streams):
        sb = [_dot_nt(k, q) if bias is None else _dot_nt(k, q) + jnp.concatenate([bias] * nh, axis=1)
              for k, _, q, bias, _ in streams]
        mnew = [jnp.maximum(st[0], jnp.max(s, axis=0, keepdims=True)) for s, (_, _, _, _, st) in zip(sb, streams)]
        p = [jnp.exp2(s - mn) for s, mn in zip(sb, mnew)]
        a = [jnp.exp2(st[0] - mn) for mn, (_, _, _, _, st) in zip(mnew, streams)]
        pv = [_dot(vt, pp.astype(BF16)) for pp, (_, vt, _, _, _) in zip(p, streams)]
        return [(mn, aa * st[1] + jnp.sum(pp, axis=0, keepdims=True), aa * st[2] + o)
                for mn, aa, pp, o, (_, _, _, _, st) in zip(mnew, a, p, pv, streams)]

    init = (jnp.full((1, nh * t), NEG, F32), jnp.zeros((1, nh * t), F32), jnp.zeros((HEAD_DIM, nh * t), F32))
    per_tile = t // SEL_BLOCK
    rows_g = lambda g: slice(g * HEAD_DIM, (g + 1) * HEAD_DIM)

    def slc_streams(j, jc, state, bias):
        onehot = (lane == j * per_tile + krow // SEL_BLOCK) | ((lane == NSA_BIAS_LANE) & (j < 0))
        r0 = pl.multiple_of(jc * t, t)
        kx = jnp.concatenate([ks_ref[0, pl.ds(r0, t), :], jnp.where(onehot, 1.0, 0.0).astype(BF16)], axis=1)
        return [(kx, vst_ref[0, jc, rows_g(g), :], qx[g], bias, state[g]) for g in range(2)]

    slc = [init, init]
    win = [init, init]
    n_win = WINDOW // t + 1
    for d in range(n_win):
        j = i - d
        jc = jnp.maximum(j, 0)
        kpos = j * t + krow
        wbias = jnp.where((kpos >= 0) & (kpos <= tpos) & (tpos - kpos < WINDOW), 0.0, NEG)
        causal = jnp.where(kpos <= tpos, 0.0, NEG) if d == 0 else None
        r0 = pl.multiple_of(jc * t, t)
        wstreams = [(kw_ref[0, pl.ds(r0, t), :], vwt_ref[0, jc, rows_g(g), :], qg[g], wbias, win[g]) for g in range(2)]
        res = attend(slc_streams(j, jc, slc, causal) + wstreams)
        slc, win = res[:2], res[2:]

    def slc_body(it, state):
        j = i - it
        return tuple(attend(slc_streams(j, j, state, None)))

    slc = lax.fori_loop(n_win, i + 1, slc_body, tuple(slc))

    gates = _sigmoid(gt_ref[0]).T
    for m in range(nh):
        halves = []
        for g in range(2):
            c = g * 3 * nh + m * 3
            sl = slice(m * t, (m + 1) * t)
            o_s = slc[g][2][:, sl] * (1.0 / slc[g][1][:, sl])
            o_w = win[g][2][:, sl] * (1.0 / win[g][1][:, sl])
            halves.append(gates[c:c + 1, :] * o_cmp[g][:, sl] + gates[c + 1:c + 2, :] * o_s + gates[c + 2:c + 3, :] * o_w)
        o_ref[0, :, m * LANES:(m + 1) * LANES] = jnp.concatenate(halves, axis=0).T.astype(o_ref.dtype)


def _nsa_attention(qk, kc2, vct, gt, n_cmp, t=256):
    b, s, _ = qk.shape
    assert WINDOW % t == 0 and t % SEL_BLOCK == 0
    nt = s // t
    tiles_t = lambda c0: jnp.swapaxes(qk[:, :, c0:c0 + LANES].reshape(b, nt, t, LANES), 2, 3)
    kspec = lambda col: pl.BlockSpec((1, s, LANES), lambda bi, i: (bi, 0, col))
    vspec = pl.BlockSpec((1, nt, LANES, t), lambda bi, i: (bi, 0, 0, 0))
    cspec = pl.BlockSpec((1, LANES, LANES), lambda bi, i: (bi, 0, 0))
    return pl.pallas_call(
        functools.partial(_nsa_kernel, t=t, n_cmp=n_cmp),
        out_shape=jax.ShapeDtypeStruct((b, s, NSA_WIDTH), BF16),
        grid=(b, nt),
        in_specs=[pl.BlockSpec((1, t, NSA_WIDTH), lambda bi, i: (bi, i, 0)),
                  kspec(4), vspec, kspec(6), vspec, cspec, cspec,
                  pl.BlockSpec((1, t, LANES), lambda bi, i: (bi, i, 0))],
        out_specs=pl.BlockSpec((1, t, NSA_WIDTH), lambda bi, i: (bi, i, 0)),
        compiler_params=_cparams("parallel", "arbitrary"),
        name="nsa_attention",
    )(qk, qk, tiles_t(5 * LANES), qk, tiles_t(7 * LANES), kc2, vct, gt)


def _gmlp_kernel(u_ref, v_ref, lg_ref, lb_ref, ws_ref, bs_ref, o_ref, *, tc):
    cg = GMLP_WIDTH // GMLP_GROUPS
    u = _gelu_tanh(u_ref[0])
    v = _layer_norm(_gelu_tanh(v_ref[0]), lg_ref[...], lb_ref[...]).astype(BF16)
    rr = lax.broadcasted_iota(jnp.int32, (GMLP_CHUNK, GMLP_CHUNK), 0)
    cc = lax.broadcasted_iota(jnp.int32, (GMLP_CHUNK, GMLP_CHUNK), 1)
    for g in range(GMLP_GROUPS):
        wm = jnp.where(rr >= cc, ws_ref[g], 0.0).astype(BF16)
        bias = bs_ref[:, g:g + 1]
        for c0 in range(0, tc, GMLP_CHUNK):
            mixed = _dot(wm, v[c0:c0 + GMLP_CHUNK, g * cg:(g + 1) * cg]) + bias
            o_ref[0, c0:c0 + GMLP_CHUNK, g * cg:(g + 1) * cg] = (
                u[c0:c0 + GMLP_CHUNK, g * cg:(g + 1) * cg] * mixed).astype(o_ref.dtype)


def _gmlp(uv, ln_g, ln_b, ws, bs, tc=512):
    b, s, _ = uv.shape
    vec = lambda a: a.reshape(1, GMLP_WIDTH).astype(F32)
    return pl.pallas_call(
        functools.partial(_gmlp_kernel, tc=tc),
        out_shape=jax.ShapeDtypeStruct((b, s, GMLP_WIDTH), BF16),
        grid=(b, s // tc),
        in_specs=[pl.BlockSpec((1, tc, GMLP_WIDTH), lambda bi, c: (bi, c, 0)),
                  pl.BlockSpec((1, tc, GMLP_WIDTH), lambda bi, c: (bi, c, 1)),
                  pl.BlockSpec((1, GMLP_WIDTH), lambda bi, c: (0, 0)),
                  pl.BlockSpec((1, GMLP_WIDTH), lambda bi, c: (0, 0)),
                  pl.BlockSpec((GMLP_GROUPS, GMLP_CHUNK, GMLP_CHUNK), lambda bi, c: (0, 0, 0)),
                  pl.BlockSpec((GMLP_CHUNK, GMLP_GROUPS), lambda bi, c: (0, 0))],
        out_specs=pl.BlockSpec((1, tc, GMLP_WIDTH), lambda bi, c: (bi, c, 0)),
        compiler_params=_cparams("parallel", "parallel"),
        name="gmlp",
    )(uv, uv, vec(ln_g), vec(ln_b), ws.astype(F32), jnp.transpose(bs).astype(F32))


def _head_pair_perm():
    m, g, d = np.meshgrid(np.arange(4), np.arange(2), np.arange(HEAD_DIM), indexing="ij")
    return ((g * 4 + m) * HEAD_DIM + d).reshape(-1)


def _even_layer(x, w_in, conv_w, conv_b, conv_ln_g, conv_ln_b, w_out, post, b, s):
    w_sc = jnp.concatenate([w_in[:, :SB_WIDTH] * (QK_SCALE * LOG2E), w_in[:, SB_WIDTH:]], axis=1)
    qkv, ag = _proj(x, w_sc.astype(BF16), [(0, 3 * SB_WIDTH), (3 * SB_WIDTH, 2 * CONV_CH)], [BF16, F32])
    o_sb = _sb_attention(qkv.reshape(b, s, -1))
    o_cv = _conformer_conv(ag.reshape(b, s, -1), conv_w, conv_b, conv_ln_g, conv_ln_b)
    return _post_mixer(o_sb.reshape(b * s, -1), o_cv.reshape(b * s, -1), w_out.astype(BF16), x, *post)


def _odd_layer(x, w_in, cmpk_pos, cmpk_w1, cmpk_w2, cmpv_pos, cmpv_w1, cmpv_w2,
               gmlp_ln_g, gmlp_ln_b, gmlp_ws, gmlp_bs, w_out, post, b, s):
    perm = _head_pair_perm()
    o_kc, o_ks, o_gt, o_u = NSA_WIDTH, NSA_WIDTH + 2 * KV_WIDTH, NSA_WIDTH + 6 * KV_WIDTH, NSA_WIDTH + 6 * KV_WIDTH + 24
    w_gt = jnp.pad(w_in[:, o_gt:o_u], ((0, 0), (0, LANES - 24)))
    w_re = jnp.concatenate([w_in[:, perm] * (QK_SCALE * LOG2E), w_in[:, o_ks:o_gt], w_in[:, o_kc:o_ks], w_in[:, o_u:], w_gt], axis=1)
    qk, cv, uv, gt = _proj(x, w_re.astype(BF16),
                           [(0, 1024), (1024, 256), (1280, 1024), (2304, LANES)], [BF16, F32, F32, F32])
    n_cmp = (s - CMP_BLOCK) // CMP_STRIDE + 1
    kc2, vct = _compress(cv.reshape(b, s, -1), cmpk_pos, cmpv_pos, cmpk_w1, cmpv_w1, cmpk_w2, cmpv_w2)
    o_nsa = _nsa_attention(qk.reshape(b, s, -1), kc2, vct, gt.reshape(b, s, -1), n_cmp)
    o_mlp = _gmlp(uv.reshape(b, s, -1), gmlp_ln_g, gmlp_ln_b, gmlp_ws, gmlp_bs)
    w_out_re = jnp.concatenate([w_out[:NSA_WIDTH][perm], w_out[NSA_WIDTH:]], axis=0)
    return _post_mixer(o_nsa.reshape(b * s, -1), o_mlp.reshape(b * s, -1), w_out_re.astype(BF16), x, *post)


def kernel(x, ev_w_in, ev_conv_w, ev_conv_b, ev_conv_ln_g, ev_conv_ln_b, ev_w_out, od_w_in, od_cmpk_pos, od_cmpk_w1, od_cmpk_w2, od_cmpv_pos, od_cmpv_w1, od_cmpv_w2, od_gmlp_ln_g, od_gmlp_ln_b, od_gmlp_ws, od_gmlp_bs, od_w_out, ffn_w_gate, ffn_w_up, ffn_w_down, ln1_g, ln1_b, ln2_g, ln2_b):
    b, s, d = x.shape
    assert s // CMP_STRIDE == LANES and s // SEL_BLOCK == SEL_LANES
    h = x.reshape(b * s, d)
    for layer in range(DEPTH):
        i = layer // 2
        post = (ln1_g[layer], ln1_b[layer], ffn_w_gate[layer].astype(BF16), ffn_w_up[layer].astype(BF16),
                ffn_w_down[layer].astype(BF16), ln2_g[layer], ln2_b[layer])
        if layer % 2 == 0:
            h = _even_layer(h, ev_w_in[i], ev_conv_w[i], ev_conv_b[i], ev_conv_ln_g[i], ev_conv_ln_b[i],
                            ev_w_out[i], post, b, s)
        else:
            h = _odd_layer(h, od_w_in[i], od_cmpk_pos[i], od_cmpk_w1[i], od_cmpk_w2[i],
                           od_cmpv_pos[i], od_cmpv_w1[i], od_cmpv_w2[i], od_gmlp_ln_g[i], od_gmlp_ln_b[i],
                           od_gmlp_ws[i], od_gmlp_bs[i], od_w_out[i], post, b, s)
    return h.reshape(b, s, d)
```

```python
import functools

import numpy as np
import jax
import jax.numpy as jnp
from jax import lax
from jax.experimental import pallas as pl
from jax.experimental.pallas import tpu as pltpu

F32 = jnp.float32
BF16 = jnp.bfloat16

D_MODEL = 1024
DEPTH = 2
HEAD_DIM = 64
LANES = 128
SUBLANES = 8
SB_WIDTH = 512
SB_HEADS = 8
CONV_CH = 512
CONV_WIDTH = 31
NSA_WIDTH = 512
KV_WIDTH = 128
CMP_BLOCK = 32
CMP_STRIDE = 16
SEL_BLOCK = 64
SEL_TOPK = 8
SEL_LANES = 32
WINDOW = 512
GMLP_WIDTH = 512
GMLP_GROUPS = 4
GMLP_CHUNK = 128
D_FF = 2816
ALPHA = (2 * DEPTH) ** 0.25
LN_EPS = 1e-5
NEG = -1e30
FORCE_BONUS = 1e3
QK_SCALE = HEAD_DIM ** -0.5
LOG2E = float(np.log2(np.e))
NSA_BIAS_LANE = SEL_LANES

VMEM_LIMIT = 56 * 1024 * 1024


def _cparams(*sem):
    return pltpu.CompilerParams(dimension_semantics=sem, vmem_limit_bytes=VMEM_LIMIT)


def _layer_norm(y, g, b):
    mu = jnp.mean(y, axis=-1, keepdims=True)
    d = y - mu
    var = jnp.mean(d * d, axis=-1, keepdims=True)
    return d * lax.rsqrt(var + LN_EPS) * g + b


def _sigmoid(x):
    return 1.0 / (1.0 + jnp.exp(-x))


def _gelu_tanh(x):
    c = np.float32(np.sqrt(2.0 / np.pi))
    return x * (0.5 * (1.0 + jnp.tanh(c * (x + 0.044715 * (x * x * x)))))


def _dot(a, b):
    return jnp.dot(a, b, preferred_element_type=F32)


def _dot_nt(a, b):
    return lax.dot_general(a, b, (((1,), (1,)), ((), ())), preferred_element_type=F32)


KEY_TILE = 256


def _proj_kernel(x_ref, w_ref, *out_refs, splits, n_plain):
    xb = x_ref[...].astype(BF16)
    for n, ((start, width), o_ref) in enumerate(zip(splits, out_refs)):
        y = _dot(xb, w_ref[:, start:start + width]).astype(o_ref.dtype)
        if n < n_plain:
            o_ref[...] = y
        else:
            for s in range(o_ref.shape[0]):
                o_ref[s] = y[s * KEY_TILE:(s + 1) * KEY_TILE, :].astype(F32).T.astype(o_ref.dtype)


def _proj(x, w, splits, dtypes, n_plain, tm=512):
    m, k = x.shape
    n = w.shape[1]
    plain, tposed = splits[:n_plain], splits[n_plain:]
    return pl.pallas_call(
        functools.partial(_proj_kernel, splits=tuple(splits), n_plain=n_plain),
        out_shape=[jax.ShapeDtypeStruct((m, wd), dt) for (_, wd), dt in zip(plain, dtypes)]
        + [jax.ShapeDtypeStruct((m // KEY_TILE, wd, KEY_TILE), dt) for (_, wd), dt in zip(tposed, dtypes[n_plain:])],
        grid=(m // tm,),
        in_specs=[pl.BlockSpec((tm, k), lambda i: (i, 0)),
                  pl.BlockSpec((k, n), lambda i: (0, 0))],
        out_specs=[pl.BlockSpec((tm, wd), lambda i: (i, 0)) for (_, wd) in plain]
        + [pl.BlockSpec((tm // KEY_TILE, wd, KEY_TILE), lambda i: (i, 0, 0)) for (_, wd) in tposed],
        compiler_params=_cparams("parallel"),
        name="proj_in",
    )(x, w)


SB_UNDERFLOW = 150.0


def _sb_kernel(q_ref, k_ref, vt_ref, o_ref, *, t, nh):
    i = pl.program_id(2)
    lane = lax.broadcasted_iota(jnp.int32, (1, LANES), 1)
    rr = lax.broadcasted_iota(jnp.int32, (t, t), 0)
    cc = lax.broadcasted_iota(jnp.int32, (t, t), 1)
    before = rr < cc
    later = jnp.where(before, 1.0, 0.0).astype(BF16)
    hs = range(nh)
    pair = lambda h: slice((h // 2) * LANES, (h // 2 + 1) * LANES)
    qh = [q_ref[0, :, pair(h)] * jnp.where(lane // HEAD_DIM == h % 2, 1.0, 0.0).astype(BF16) for h in hs]

    def tile(j, state, diag):
        r0 = pl.multiple_of(j * t, t)
        z = [_dot_nt(k_ref[0, pl.ds(r0, t), pair(h)], qh[h]) for h in hs]
        lg = [jnp.log2(1.0 + jnp.exp2(-jnp.abs(z[h]))) for h in hs]
        lr = [jnp.minimum(-z[h], 0.0) - lg[h] for h in hs]
        if diag:
            lr = [jnp.where(before, lr[h], 0.0) for h in hs]
        hi = [lr[h].astype(BF16) for h in hs]
        hl = [jnp.concatenate([hi[h], (lr[h] - hi[h].astype(F32)).astype(BF16)], axis=1) for h in hs]
        bt = [_dot(later, hl[h]) for h in hs]
        w = [jnp.exp2(jnp.minimum(z[h], 0.0) - lg[h] + (bt[h][:, :t] + bt[h][:, t:] + state[h][0])) for h in hs]
        if diag:
            w = [jnp.where(before, w[h], 0.0) for h in hs]
        pv = [_dot(vt_ref[0, j, h * HEAD_DIM:(h + 1) * HEAD_DIM, :], w[h].astype(BF16)) for h in hs]
        return tuple((state[h][0] + jnp.sum(lr[h], axis=0, keepdims=True), state[h][1] + pv[h]) for h in hs)

    zero = (jnp.zeros((1, t), F32), jnp.zeros((HEAD_DIM, t), F32))
    state = tile(i, (zero,) * nh, True)

    def cond(c):
        return (c[0] >= 0) & (c[1] > -SB_UNDERFLOW)

    def body(c):
        st = tile(c[0], c[2], False)
        top = st[0][0]
        for h in range(1, nh):
            top = jnp.maximum(top, st[h][0])
        return c[0] - 1, jnp.max(top), st

    _, _, state = lax.while_loop(cond, body, (i - 1, jnp.float32(0.0), state))
    for p in range(nh // 2):
        o_ref[0, :, p * LANES:(p + 1) * LANES] = jnp.concatenate(
            [state[2 * p][1], state[2 * p + 1][1]], axis=0).T.astype(o_ref.dtype)


def _sb_attention(qk, vt, nh=SB_HEADS):
    b, s, _ = qk.shape
    t = vt.shape[-1]
    wb = nh * HEAD_DIM
    nblk = SB_WIDTH // wb
    return pl.pallas_call(
        functools.partial(_sb_kernel, t=t, nh=nh),
        out_shape=jax.ShapeDtypeStruct((b, s, SB_WIDTH), BF16),
        grid=(b, nblk, s // t),
        in_specs=[pl.BlockSpec((1, t, wb), lambda bi, hb, i: (bi, i, hb)),
                  pl.BlockSpec((1, s, wb), lambda bi, hb, i: (bi, 0, nblk + hb)),
                  pl.BlockSpec((1, s // t, wb, t), lambda bi, hb, i: (bi, 0, hb, 0))],
        out_specs=pl.BlockSpec((1, t, wb), lambda bi, hb, i: (bi, i, hb)),
        compiler_params=_cparams("parallel", "parallel", "arbitrary"),
        name="sb_attention",
    )(qk, qk, vt)


CONV_PAD = 32


def _conv_kernel(a_ref, g_ref, w_ref, b_ref, lg_ref, lb_ref, o_ref, hbuf, *, tc, sub):
    c = pl.program_id(1)

    @pl.when(c == 0)
    def _():
        hbuf[0:CONV_PAD, :] = jnp.zeros((CONV_PAD, CONV_CH), F32)

    hbuf[CONV_PAD:CONV_PAD + tc, :] = a_ref[0] * _sigmoid(g_ref[0])
    off = CONV_PAD - (CONV_WIDTH - 1)
    for r0 in range(0, tc, sub):
        acc = jnp.zeros((sub, CONV_CH), F32) + b_ref[...]
        for ph in range(SUBLANES):
            taps = [w for w in range(CONV_WIDTH) if (off + w) % SUBLANES == ph]
            rows = sub + (SUBLANES if ph else 0)
            part = None
            for w in taps:
                a0 = r0 + (off + w) // SUBLANES * SUBLANES
                term = w_ref[w:w + 1, :] * hbuf[a0:a0 + rows, :]
                part = term if part is None else part + term
            acc = acc + part[ph:ph + sub, :]
        y = _layer_norm(acc, lg_ref[...], lb_ref[...])
        o_ref[0, r0:r0 + sub, :] = (y * _sigmoid(y)).astype(o_ref.dtype)
    hbuf[0:CONV_PAD, :] = hbuf[tc:tc + CONV_PAD, :]


def _conformer_conv(ag, w_dw, b_dw, ln_g, ln_b, tc=256, sub=64):
    b, s, _ = ag.shape
    vec = lambda v: v.reshape(1, CONV_CH).astype(F32)
    return pl.pallas_call(
        functools.partial(_conv_kernel, tc=tc, sub=sub),
        out_shape=jax.ShapeDtypeStruct((b, s, CONV_CH), BF16),
        grid=(b, s // tc),
        in_specs=[pl.BlockSpec((1, tc, CONV_CH), lambda bi, c: (bi, c, 0)),
                  pl.BlockSpec((1, tc, CONV_CH), lambda bi, c: (bi, c, 1)),
                  pl.BlockSpec((CONV_WIDTH, CONV_CH), lambda bi, c: (0, 0)),
                  pl.BlockSpec((1, CONV_CH), lambda bi, c: (0, 0)),
                  pl.BlockSpec((1, CONV_CH), lambda bi, c: (0, 0)),
                  pl.BlockSpec((1, CONV_CH), lambda bi, c: (0, 0))],
        out_specs=pl.BlockSpec((1, tc, CONV_CH), lambda bi, c: (bi, c, 0)),
        scratch_shapes=[pltpu.VMEM((CONV_PAD + tc, CONV_CH), F32)],
        compiler_params=_cparams("parallel", "arbitrary"),
        name="conformer_conv",
    )(ag, ag, w_dw.astype(F32), vec(b_dw), vec(ln_g), vec(ln_b))


FF_CHUNK = 256


def _post_mixer_kernel(l_ref, r_ref, wo_ref, x_ref, g1_ref, b1_ref, wg_ref, wu_ref, wd_ref, g2_ref, b2_ref, o_ref):
    kl = l_ref.shape[1]
    m = _dot(l_ref[...], wo_ref[0:kl, :]) + _dot(r_ref[...], wo_ref[kl:, :])
    x = _layer_norm(ALPHA * x_ref[...] + m, g1_ref[...], b1_ref[...])
    xb = x.astype(BF16)
    acc = jnp.zeros(x.shape, F32)
    for c0 in range(0, D_FF, FF_CHUNK):
        gate = _dot(xb, wg_ref[:, c0:c0 + FF_CHUNK])
        up = _dot(xb, wu_ref[:, c0:c0 + FF_CHUNK])
        h = (gate * _sigmoid(gate) * up).astype(BF16)
        acc = acc + _dot(h, wd_ref[c0:c0 + FF_CHUNK, :])
    o_ref[...] = _layer_norm(ALPHA * x + acc, g2_ref[...], b2_ref[...])


def _post_mixer(left, right, wo, x, g1, b1, wg, wu, wd, g2, b2, tm=512):
    m, d = x.shape
    kl, kr = left.shape[1], right.shape[1]
    ff = wg.shape[1]
    row = lambda w: pl.BlockSpec((tm, w), lambda i: (i, 0))
    res = lambda shape: pl.BlockSpec(shape, lambda i: (0, 0), pipeline_mode=pl.Buffered(1))
    vec = lambda v: v.reshape(1, d).astype(F32)
    return pl.pallas_call(
        _post_mixer_kernel,
        out_shape=jax.ShapeDtypeStruct((m, d), F32),
        grid=(m // tm,),
        in_specs=[row(kl), row(kr), res((kl + kr, d)), row(d), res((1, d)), res((1, d)),
                  res((d, ff)), res((d, ff)), res((ff, d)), res((1, d)), res((1, d))],
        out_specs=row(d),
        compiler_params=_cparams("parallel"),
        name="post_mixer",
    )(left, right, wo, x, vec(g1), vec(b1), wg, wu, wd, vec(g2), vec(b2))


def _compress_kernel(xk_ref, xv_ref, pos_ref, w1_ref, w2_ref, kc_ref, vct_ref):
    n = xk_ref.shape[1] // CMP_STRIDE
    out = []
    for s, x_ref in enumerate((xk_ref, xv_ref)):
        xs = [x_ref[0, pl.ds(r, n, stride=CMP_STRIDE), :] for r in range(CMP_STRIDE)]
        half = lambda o: jnp.concatenate([x + pos_ref[s, o + r:o + r + 1, :] for r, x in enumerate(xs)], axis=1)
        top = _dot(half(0).astype(BF16), w1_ref[s, 0])
        bot = _dot(half(CMP_STRIDE).astype(BF16), w1_ref[s, 1])
        h = top + pltpu.roll(bot, n - 1, 0)
        out.append(_dot(_gelu_tanh(h).astype(BF16), w2_ref[s]))
    kc_ref[0] = out[0].astype(kc_ref.dtype)
    vct_ref[0] = out[1].astype(vct_ref.dtype).astype(F32).T.astype(vct_ref.dtype)


def _block_diag2(a):
    z = jnp.zeros_like(a)
    return jnp.concatenate([jnp.concatenate([a, z], axis=-1), jnp.concatenate([z, a], axis=-1)], axis=-2)


def _compress(cv, pos_k, pos_v, w1k, w1v, w2k, w2v):
    b, s, _ = cv.shape
    n = s // CMP_STRIDE
    pos = jnp.stack([jnp.concatenate([p, p], axis=-1) for p in (pos_k, pos_v)]).astype(F32)
    w1 = jnp.stack([_block_diag2(w.reshape(CMP_BLOCK, HEAD_DIM, HEAD_DIM)) for w in (w1k, w1v)])
    w1 = w1.reshape(2, 2, CMP_STRIDE * KV_WIDTH, KV_WIDTH).astype(BF16)
    w2 = jnp.stack([_block_diag2(w) for w in (w2k, w2v)]).astype(BF16)
    full = lambda a: pl.BlockSpec(a.shape, lambda bi: (0,) * a.ndim)
    return pl.pallas_call(
        _compress_kernel,
        out_shape=[jax.ShapeDtypeStruct((b, n, KV_WIDTH), BF16), jax.ShapeDtypeStruct((b, KV_WIDTH, n), BF16)],
        grid=(b,),
        in_specs=[pl.BlockSpec((1, s, KV_WIDTH), lambda bi: (bi, 0, 0)),
                  pl.BlockSpec((1, s, KV_WIDTH), lambda bi: (bi, 0, 1)), full(pos), full(w1), full(w2)],
        out_specs=[pl.BlockSpec((1, n, KV_WIDTH), lambda bi: (bi, 0, 0)),
                   pl.BlockSpec((1, KV_WIDTH, n), lambda bi: (bi, 0, 0))],
        compiler_params=_cparams("parallel"),
        name="nsa_compress",
    )(cv, cv, pos, w1, w2)


NSA_GROUP = 4


def _nsa_kernel(q_ref, ks_ref, vst_ref, kw_ref, vwt_ref, kc_ref, vct_ref, gt_ref, o_ref, *, t, n_cmp):
    i = pl.program_id(1)
    nh = NSA_GROUP
    lane = lax.broadcasted_iota(jnp.int32, (1, LANES), 1)
    tpos = i * t + lax.broadcasted_iota(jnp.int32, (1, t), 1)
    krow = lax.broadcasted_iota(jnp.int32, (t, 1), 0)
    qg = []
    for g in range(2):
        hm = jnp.where(lane // HEAD_DIM == g, 1.0, 0.0).astype(BF16)
        qg.append(jnp.concatenate([q_ref[0, :, m * LANES:(m + 1) * LANES] * hm for m in range(nh)], axis=0))

    nblk = lax.broadcasted_iota(jnp.int32, (LANES, 1), 0)
    cvalid = (nblk * CMP_STRIDE + (CMP_BLOCK - 1) <= tpos) & (nblk < n_cmp)
    jr = lax.broadcasted_iota(jnp.int32, (SEL_LANES, LANES), 0)
    nc = lax.broadcasted_iota(jnp.int32, (SEL_LANES, LANES), 1)
    ovl_t = jnp.where((nc * CMP_STRIDE < jr * SEL_BLOCK + SEL_BLOCK) & (nc * CMP_STRIDE + CMP_BLOCK > jr * SEL_BLOCK)
                      & (nc < n_cmp), 1.0, 0.0).astype(BF16)
    jsel = lax.broadcasted_iota(jnp.int32, (SEL_LANES, 1), 0)
    blk_t = tpos // SEL_BLOCK
    forced = (jsel == 0) | (jsel == blk_t) | (jsel == blk_t - 1)
    pad_rows = lax.broadcasted_iota(jnp.int32, (LANES - SEL_LANES, 1), 0)
    o_cmp = []
    qx = []
    for g in range(2):
        st = _dot_nt(kc_ref[0], qg[g])
        ps = []
        psum = jnp.zeros((LANES, t), F32)
        for m in range(nh):
            s = jnp.where(cvalid, st[:, m * t:(m + 1) * t], NEG)
            e = jnp.where(cvalid, jnp.exp2(s - jnp.max(s, axis=0, keepdims=True)), 0.0)
            l = jnp.sum(e, axis=0, keepdims=True)
            p = e * (1.0 / jnp.where(l > 0.0, l, 1.0))
            ps.append(p.astype(BF16))
            psum = psum + p
        o_cmp.append(_dot(vct_ref[0, g * HEAD_DIM:(g + 1) * HEAD_DIM, :], jnp.concatenate(ps, axis=1)))
        hi = psum.astype(BF16)
        lo = (psum - hi.astype(F32)).astype(BF16)
        it = _dot(ovl_t, jnp.concatenate([hi, lo], axis=1))
        imp = it[:, :t] + it[:, t:]
        score = jnp.where(jsel <= blk_t, imp + jnp.where(forced, FORCE_BONUS, 0.0), NEG)
        cnt = jnp.zeros((SEL_LANES, t), F32)
        for c in range(SEL_LANES):
            row = score[c:c + 1, :]
            cnt = cnt + jnp.where((row > score) | ((row == score) & (jsel > c)), 1.0, 0.0)
        tail = jnp.where(pad_rows == NSA_BIAS_LANE - SEL_LANES, NEG, 0.0) + jnp.zeros((1, t), F32)
        selt = jnp.concatenate([jnp.where(cnt < SEL_TOPK, 0.0, NEG), tail], axis=0).T.astype(BF16)
        qx.append(jnp.concatenate([qg[g], jnp.concatenate([selt] * nh, axis=0)], axis=1))

    def attend(streams, scores=None):
        if scores is None:
            scores = [_dot_nt(k, q) for k, _, q, _, _ in streams]
        sb = [s if bias is None else s + jnp.concatenate([bias] * nh, axis=1)
              for s, (_, _, _, bias, _) in zip(scores, streams)]
        mnew = [jnp.maximum(st[0], jnp.max(s, axis=0, keepdims=True)) for s, (_, _, _, _, st) in zip(sb, streams)]
        p = [jnp.exp2(s - mn) for s, mn in zip(sb, mnew)]
        a = [jnp.exp2(st[0] - mn) for mn, (_, _, _, _, st) in zip(mnew, streams)]
        pv = [_dot(vt, pp.astype(BF16)) for pp, (_, vt, _, _, _) in zip(p, streams)]
        return [(mn, aa * st[1] + jnp.sum(pp, axis=0, keepdims=True), aa * st[2] + o)
                for mn, aa, pp, o, (_, _, _, _, st) in zip(mnew, a, p, pv, streams)]

    init = (jnp.full((1, nh * t), NEG, F32), jnp.zeros((1, nh * t), F32), jnp.zeros((HEAD_DIM, nh * t), F32))
    per_tile = t // SEL_BLOCK
    rows_g = lambda g: slice(g * HEAD_DIM, (g + 1) * HEAD_DIM)

    def slc_streams(j, jc, state, bias):
        onehot = (lane == j * per_tile + krow // SEL_BLOCK) | ((lane == NSA_BIAS_LANE) & (j < 0))
        r0 = pl.multiple_of(jc * t, t)
        kx = jnp.concatenate([ks_ref[0, pl.ds(r0, t), :], jnp.where(onehot, 1.0, 0.0).astype(BF16)], axis=1)
        return [(kx, vst_ref[0, jc, rows_g(g), :], qx[g], bias, state[g]) for g in range(2)]

    slc = [init, init]
    win = [init, init]
    n_win = WINDOW // t + 1
    for d in range(n_win):
        j = i - d
        jc = jnp.maximum(j, 0)
        kpos = j * t + krow
        wbias = jnp.where((kpos >= 0) & (kpos <= tpos) & (tpos - kpos < WINDOW), 0.0, NEG)
        causal = jnp.where(kpos <= tpos, 0.0, NEG) if d == 0 else None
        r0 = pl.multiple_of(jc * t, t)
        sstreams = slc_streams(j, jc, slc, causal)
        wstreams = [(None, vwt_ref[0, jc, rows_g(g), :], None, wbias, win[g]) for g in range(2)]
        kwx = jnp.concatenate([kw_ref[0, pl.ds(r0, t), :], jnp.zeros((t, LANES), BF16)], axis=1)
        both = [_dot_nt(jnp.concatenate([sstreams[g][0], kwx], axis=0), qx[g]) for g in range(2)]
        res = attend(sstreams + wstreams, [both[0][:t], both[1][:t], both[0][t:], both[1][t:]])
        slc, win = res[:2], res[2:]

    def slc_body(it, state):
        j = i - it
        return tuple(attend(slc_streams(j, j, state, None)))

    slc = lax.fori_loop(n_win, i + 1, slc_body, tuple(slc))

    gates = _sigmoid(gt_ref[0]).T
    for m in range(nh):
        halves = []
        for g in range(2):
            c = g * 3 * nh + m * 3
            sl = slice(m * t, (m + 1) * t)
            o_s = slc[g][2][:, sl] * (1.0 / slc[g][1][:, sl])
            o_w = win[g][2][:, sl] * (1.0 / win[g][1][:, sl])
            halves.append(gates[c:c + 1, :] * o_cmp[g][:, sl] + gates[c + 1:c + 2, :] * o_s + gates[c + 2:c + 3, :] * o_w)
        o_ref[0, :, m * LANES:(m + 1) * LANES] = jnp.concatenate(halves, axis=0).T.astype(o_ref.dtype)


def _nsa_attention(qk, vt, kc2, vct, gt, n_cmp):
    b, s, _ = qk.shape
    t = vt.shape[-1]
    assert WINDOW % t == 0 and t % SEL_BLOCK == 0
    nt = s // t
    kspec = lambda col: pl.BlockSpec((1, s, LANES), lambda bi, i: (bi, 0, col))
    vspec = lambda blk: pl.BlockSpec((1, nt, LANES, t), lambda bi, i: (bi, 0, blk, 0))
    cspec = pl.BlockSpec((1, LANES, LANES), lambda bi, i: (bi, 0, 0))
    return pl.pallas_call(
        functools.partial(_nsa_kernel, t=t, n_cmp=n_cmp),
        out_shape=jax.ShapeDtypeStruct((b, s, NSA_WIDTH), BF16),
        grid=(b, nt),
        in_specs=[pl.BlockSpec((1, t, NSA_WIDTH), lambda bi, i: (bi, i, 0)),
                  kspec(4), vspec(0), kspec(5), vspec(1), cspec, cspec,
                  pl.BlockSpec((1, t, LANES), lambda bi, i: (bi, i, 0))],
        out_specs=pl.BlockSpec((1, t, NSA_WIDTH), lambda bi, i: (bi, i, 0)),
        compiler_params=_cparams("parallel", "arbitrary"),
        name="nsa_attention",
    )(qk, qk, vt, qk, vt, kc2, vct, gt)


def _gmlp_kernel(u_ref, v_ref, lg_ref, lb_ref, ws_ref, bs_ref, o_ref, *, tc):
    cg = GMLP_WIDTH // GMLP_GROUPS
    u = _gelu_tanh(u_ref[0])
    v = _layer_norm(_gelu_tanh(v_ref[0]), lg_ref[...], lb_ref[...]).astype(BF16)
    rr = lax.broadcasted_iota(jnp.int32, (GMLP_CHUNK, GMLP_CHUNK), 0)
    cc = lax.broadcasted_iota(jnp.int32, (GMLP_CHUNK, GMLP_CHUNK), 1)
    for g in range(GMLP_GROUPS):
        wm = jnp.where(rr >= cc, ws_ref[g], 0.0).astype(BF16)
        bias = bs_ref[:, g:g + 1]
        for c0 in range(0, tc, GMLP_CHUNK):
            mixed = _dot(wm, v[c0:c0 + GMLP_CHUNK, g * cg:(g + 1) * cg]) + bias
            o_ref[0, c0:c0 + GMLP_CHUNK, g * cg:(g + 1) * cg] = (
                u[c0:c0 + GMLP_CHUNK, g * cg:(g + 1) * cg] * mixed).astype(o_ref.dtype)


def _gmlp(uv, ln_g, ln_b, ws, bs, tc=512):
    b, s, _ = uv.shape
    vec = lambda a: a.reshape(1, GMLP_WIDTH).astype(F32)
    return pl.pallas_call(
        functools.partial(_gmlp_kernel, tc=tc),
        out_shape=jax.ShapeDtypeStruct((b, s, GMLP_WIDTH), BF16),
        grid=(b, s // tc),
        in_specs=[pl.BlockSpec((1, tc, GMLP_WIDTH), lambda bi, c: (bi, c, 0)),
                  pl.BlockSpec((1, tc, GMLP_WIDTH), lambda bi, c: (bi, c, 1)),
                  pl.BlockSpec((1, GMLP_WIDTH), lambda bi, c: (0, 0)),
                  pl.BlockSpec((1, GMLP_WIDTH), lambda bi, c: (0, 0)),
                  pl.BlockSpec((GMLP_GROUPS, GMLP_CHUNK, GMLP_CHUNK), lambda bi, c: (0, 0, 0)),
                  pl.BlockSpec((GMLP_CHUNK, GMLP_GROUPS), lambda bi, c: (0, 0))],
        out_specs=pl.BlockSpec((1, tc, GMLP_WIDTH), lambda bi, c: (bi, c, 0)),
        compiler_params=_cparams("parallel", "parallel"),
        name="gmlp",
    )(uv, uv, vec(ln_g), vec(ln_b), ws.astype(F32), jnp.transpose(bs).astype(F32))


def _head_pair_perm():
    m, g, d = np.meshgrid(np.arange(4), np.arange(2), np.arange(HEAD_DIM), indexing="ij")
    return ((g * 4 + m) * HEAD_DIM + d).reshape(-1)


def _even_layer(x, w_in, conv_w, conv_b, conv_ln_g, conv_ln_b, w_out, post, b, s):
    w_sc = jnp.concatenate([w_in[:, :SB_WIDTH] * (QK_SCALE * LOG2E), w_in[:, SB_WIDTH:]], axis=1)
    w_sc = jnp.concatenate([w_sc[:, :2 * SB_WIDTH], w_sc[:, 3 * SB_WIDTH:], w_sc[:, 2 * SB_WIDTH:3 * SB_WIDTH]], axis=1)
    qk, ag, vt = _proj(x, w_sc.astype(BF16), [(0, 2 * SB_WIDTH), (2 * SB_WIDTH, 2 * CONV_CH), (2 * SB_WIDTH + 2 * CONV_CH, SB_WIDTH)],
                       [BF16, F32, BF16], n_plain=2)
    o_sb = _sb_attention(qk.reshape(b, s, -1), vt.reshape(b, s // KEY_TILE, SB_WIDTH, KEY_TILE))
    o_cv = _conformer_conv(ag.reshape(b, s, -1), conv_w, conv_b, conv_ln_g, conv_ln_b)
    return _post_mixer(o_sb.reshape(b * s, -1), o_cv.reshape(b * s, -1), w_out.astype(BF16), x, *post)


def _odd_layer(x, w_in, cmpk_pos, cmpk_w1, cmpk_w2, cmpv_pos, cmpv_w1, cmpv_w2,
               gmlp_ln_g, gmlp_ln_b, gmlp_ws, gmlp_bs, w_out, post, b, s):
    perm = _head_pair_perm()
    o_gt, o_u = NSA_WIDTH + 6 * KV_WIDTH, NSA_WIDTH + 6 * KV_WIDTH + 24
    w_gt = jnp.pad(w_in[:, o_gt:o_u], ((0, 0), (0, LANES - 24)))
    col = lambda n: w_in[:, NSA_WIDTH + n * KV_WIDTH:NSA_WIDTH + (n + 1) * KV_WIDTH]
    w_re = jnp.concatenate([w_in[:, perm] * (QK_SCALE * LOG2E), col(2), col(4), col(0), col(1), w_in[:, o_u:], w_gt,
                            col(3), col(5)], axis=1)
    qk, cv, uv, gt, vt = _proj(
        x, w_re.astype(BF16), [(0, 768), (768, 256), (1024, 1024), (2048, LANES), (2176, 2 * KV_WIDTH)],
        [BF16, F32, F32, F32, BF16], n_plain=4)
    n_cmp = (s - CMP_BLOCK) // CMP_STRIDE + 1
    kc2, vct = _compress(cv.reshape(b, s, -1), cmpk_pos, cmpv_pos, cmpk_w1, cmpv_w1, cmpk_w2, cmpv_w2)
    o_nsa = _nsa_attention(qk.reshape(b, s, -1), vt.reshape(b, s // KEY_TILE, 2 * KV_WIDTH, KEY_TILE), kc2, vct,
                           gt.reshape(b, s, -1), n_cmp)
    o_mlp = _gmlp(uv.reshape(b, s, -1), gmlp_ln_g, gmlp_ln_b, gmlp_ws, gmlp_bs)
    w_out_re = jnp.concatenate([w_out[:NSA_WIDTH][perm], w_out[NSA_WIDTH:]], axis=0)
    return _post_mixer(o_nsa.reshape(b * s, -1), o_mlp.reshape(b * s, -1), w_out_re.astype(BF16), x, *post)


def kernel(x, ev_w_in, ev_conv_w, ev_conv_b, ev_conv_ln_g, ev_conv_ln_b, ev_w_out, od_w_in, od_cmpk_pos, od_cmpk_w1, od_cmpk_w2, od_cmpv_pos, od_cmpv_w1, od_cmpv_w2, od_gmlp_ln_g, od_gmlp_ln_b, od_gmlp_ws, od_gmlp_bs, od_w_out, ffn_w_gate, ffn_w_up, ffn_w_down, ln1_g, ln1_b, ln2_g, ln2_b):
    b, s, d = x.shape
    assert s // CMP_STRIDE == LANES and s // SEL_BLOCK == SEL_LANES
    h = x.reshape(b * s, d)
    for layer in range(DEPTH):
        i = layer // 2
        post = (ln1_g[layer], ln1_b[layer], ffn_w_gate[layer].astype(BF16), ffn_w_up[layer].astype(BF16),
                ffn_w_down[layer].astype(BF16), ln2_g[layer], ln2_b[layer])
        if layer % 2 == 0:
            h = _even_layer(h, ev_w_in[i], ev_conv_w[i], ev_conv_b[i], ev_conv_ln_g[i], ev_conv_ln_b[i],
                            ev_w_out[i], post, b, s)
        else:
            h = _odd_layer(h, od_w_in[i], od_cmpk_pos[i], od_cmpk_w1[i], od_cmpk_w2[i],
                           od_cmpv_pos[i], od_cmpv_w1[i], od_cmpv_w2[i], od_gmlp_ln_g[i], od_gmlp_ln_b[i],
                           od_gmlp_ws[i], od_gmlp_bs[i], od_w_out[i], post, b, s)
    return h.reshape(b, s, d)
```

```python
import functools

import numpy as np
import jax
import jax.numpy as jnp
from jax import lax
from jax.experimental import pallas as pl
from jax.experimental.pallas import tpu as pltpu

F32 = jnp.float32
BF16 = jnp.bfloat16

D_MODEL = 1024
DEPTH = 2
HEAD_DIM = 64
LANES = 128
SUBLANES = 8
SB_WIDTH = 512
SB_HEADS = 8
CONV_CH = 512
CONV_WIDTH = 31
NSA_WIDTH = 512
KV_WIDTH = 128
CMP_BLOCK = 32
CMP_STRIDE = 16
SEL_BLOCK = 64
SEL_TOPK = 8
SEL_LANES = 32
WINDOW = 512
GMLP_WIDTH = 512
GMLP_GROUPS = 4
GMLP_CHUNK = 128
D_FF = 2816
ALPHA = (2 * DEPTH) ** 0.25
LN_EPS = 1e-5
NEG = -1e30
FORCE_BONUS = 1e3
QK_SCALE = HEAD_DIM ** -0.5
LOG2E = float(np.log2(np.e))
NSA_BIAS_LANE = SEL_LANES

VMEM_LIMIT = 56 * 1024 * 1024


def _cparams(*sem):
    return pltpu.CompilerParams(dimension_semantics=sem, vmem_limit_bytes=VMEM_LIMIT)


def _layer_norm(y, g, b):
    mu = jnp.mean(y, axis=-1, keepdims=True)
    d = y - mu
    var = jnp.mean(d * d, axis=-1, keepdims=True)
    return d * lax.rsqrt(var + LN_EPS) * g + b


def _sigmoid(x):
    return 1.0 / (1.0 + jnp.exp(-x))


def _gelu_tanh(x):
    c = np.float32(np.sqrt(2.0 / np.pi))
    return x * (0.5 * (1.0 + jnp.tanh(c * (x + 0.044715 * (x * x * x)))))


def _dot(a, b):
    return jnp.dot(a, b, preferred_element_type=F32)


def _dot_nt(a, b):
    return lax.dot_general(a, b, (((1,), (1,)), ((), ())), preferred_element_type=F32)


KEY_TILE = 256


def _proj_kernel(x_ref, w_ref, *out_refs, splits, n_plain):
    xb = x_ref[...].astype(BF16)
    for n, ((start, width), o_ref) in enumerate(zip(splits, out_refs)):
        y = _dot(xb, w_ref[:, start:start + width]).astype(o_ref.dtype)
        if n < n_plain:
            o_ref[...] = y
        else:
            for s in range(o_ref.shape[0]):
                o_ref[s] = y[s * KEY_TILE:(s + 1) * KEY_TILE, :].astype(F32).T.astype(o_ref.dtype)


def _proj(x, w, splits, dtypes, n_plain, tm=512):
    m, k = x.shape
    n = w.shape[1]
    plain, tposed = splits[:n_plain], splits[n_plain:]
    return pl.pallas_call(
        functools.partial(_proj_kernel, splits=tuple(splits), n_plain=n_plain),
        out_shape=[jax.ShapeDtypeStruct((m, wd), dt) for (_, wd), dt in zip(plain, dtypes)]
        + [jax.ShapeDtypeStruct((m // KEY_TILE, wd, KEY_TILE), dt) for (_, wd), dt in zip(tposed, dtypes[n_plain:])],
        grid=(m // tm,),
        in_specs=[pl.BlockSpec((tm, k), lambda i: (i, 0)),
                  pl.BlockSpec((k, n), lambda i: (0, 0))],
        out_specs=[pl.BlockSpec((tm, wd), lambda i: (i, 0)) for (_, wd) in plain]
        + [pl.BlockSpec((tm // KEY_TILE, wd, KEY_TILE), lambda i: (i, 0, 0)) for (_, wd) in tposed],
        compiler_params=_cparams("parallel"),
        name="proj_in",
    )(x, w)


SB_UNDERFLOW = 150.0


def _sb_kernel(q_ref, k_ref, vt_ref, o_ref, *, t, nh):
    i = pl.program_id(2)
    lane = lax.broadcasted_iota(jnp.int32, (1, LANES), 1)
    rr = lax.broadcasted_iota(jnp.int32, (t, t), 0)
    cc = lax.broadcasted_iota(jnp.int32, (t, t), 1)
    before = rr < cc
    later = jnp.where(before, 1.0, 0.0).astype(BF16)
    hs = range(nh)
    pair = lambda h: slice((h // 2) * LANES, (h // 2 + 1) * LANES)
    qh = [q_ref[0, :, pair(h)] * jnp.where(lane // HEAD_DIM == h % 2, 1.0, 0.0).astype(BF16) for h in hs]

    def tile(j, state, diag):
        r0 = pl.multiple_of(j * t, t)
        z = [_dot_nt(k_ref[0, pl.ds(r0, t), pair(h)], qh[h]) for h in hs]
        nz = [-z[h] for h in hs]
        lg = [jnp.log2(1.0 + jnp.exp2(jnp.minimum(z[h], nz[h]))) for h in hs]
        lb = [jnp.minimum(nz[h], 0.0) - lg[h] for h in hs]
        lr = [jnp.where(before, lb[h], 0.0) for h in hs] if diag else lb
        hi = [lr[h].astype(BF16) for h in hs]
        hl = [jnp.concatenate([hi[h], (lr[h] - hi[h].astype(F32)).astype(BF16)], axis=1) for h in hs]
        bt = [_dot(later, hl[h]) for h in hs]
        w = [jnp.exp2(lb[h] + z[h] + (bt[h][:, :t] + bt[h][:, t:] + state[h][0])) for h in hs]
        if diag:
            w = [jnp.where(before, w[h], 0.0) for h in hs]
        pv = [_dot(vt_ref[0, j, h * HEAD_DIM:(h + 1) * HEAD_DIM, :], w[h].astype(BF16)) for h in hs]
        return tuple((state[h][0] + jnp.sum(lr[h], axis=0, keepdims=True), state[h][1] + pv[h]) for h in hs)

    zero = (jnp.zeros((1, t), F32), jnp.zeros((HEAD_DIM, t), F32))
    state = tile(i, (zero,) * nh, True)

    def cond(c):
        return (c[0] >= 0) & (c[1] > -SB_UNDERFLOW)

    def body(c):
        st = tile(c[0], c[2], False)
        top = st[0][0]
        for h in range(1, nh):
            top = jnp.maximum(top, st[h][0])
        return c[0] - 1, jnp.max(top), st

    _, _, state = lax.while_loop(cond, body, (i - 1, jnp.float32(0.0), state))
    for p in range(nh // 2):
        o_ref[0, :, p * LANES:(p + 1) * LANES] = jnp.concatenate(
            [state[2 * p][1], state[2 * p + 1][1]], axis=0).T.astype(o_ref.dtype)


def _sb_attention(qk, vt, nh=SB_HEADS):
    b, s, _ = qk.shape
    t = vt.shape[-1]
    wb = nh * HEAD_DIM
    nblk = SB_WIDTH // wb
    return pl.pallas_call(
        functools.partial(_sb_kernel, t=t, nh=nh),
        out_shape=jax.ShapeDtypeStruct((b, s, SB_WIDTH), BF16),
        grid=(b, nblk, s // t),
        in_specs=[pl.BlockSpec((1, t, wb), lambda bi, hb, i: (bi, i, hb)),
                  pl.BlockSpec((1, s, wb), lambda bi, hb, i: (bi, 0, nblk + hb)),
                  pl.BlockSpec((1, s // t, wb, t), lambda bi, hb, i: (bi, 0, hb, 0))],
        out_specs=pl.BlockSpec((1, t, wb), lambda bi, hb, i: (bi, i, hb)),
        compiler_params=_cparams("parallel", "parallel", "arbitrary"),
        name="sb_attention",
    )(qk, qk, vt)


CONV_PAD = 32


def _conv_kernel(a_ref, g_ref, w_ref, b_ref, lg_ref, lb_ref, o_ref, hbuf, *, tc, sub):
    c = pl.program_id(1)

    @pl.when(c == 0)
    def _():
        hbuf[0:CONV_PAD, :] = jnp.zeros((CONV_PAD, CONV_CH), F32)

    hbuf[CONV_PAD:CONV_PAD + tc, :] = a_ref[0] * _sigmoid(g_ref[0])
    off = CONV_PAD - (CONV_WIDTH - 1)
    for r0 in range(0, tc, sub):
        acc = jnp.zeros((sub, CONV_CH), F32) + b_ref[...]
        for ph in range(SUBLANES):
            taps = [w for w in range(CONV_WIDTH) if (off + w) % SUBLANES == ph]
            rows = sub + (SUBLANES if ph else 0)
            part = None
            for w in taps:
                a0 = r0 + (off + w) // SUBLANES * SUBLANES
                term = w_ref[w:w + 1, :] * hbuf[a0:a0 + rows, :]
                part = term if part is None else part + term
            acc = acc + part[ph:ph + sub, :]
        y = _layer_norm(acc, lg_ref[...], lb_ref[...])
        o_ref[0, r0:r0 + sub, :] = (y * _sigmoid(y)).astype(o_ref.dtype)
    hbuf[0:CONV_PAD, :] = hbuf[tc:tc + CONV_PAD, :]


def _conformer_conv(ag, w_dw, b_dw, ln_g, ln_b, tc=256, sub=64):
    b, s, _ = ag.shape
    vec = lambda v: v.reshape(1, CONV_CH).astype(F32)
    return pl.pallas_call(
        functools.partial(_conv_kernel, tc=tc, sub=sub),
        out_shape=jax.ShapeDtypeStruct((b, s, CONV_CH), BF16),
        grid=(b, s // tc),
        in_specs=[pl.BlockSpec((1, tc, CONV_CH), lambda bi, c: (bi, c, 0)),
                  pl.BlockSpec((1, tc, CONV_CH), lambda bi, c: (bi, c, 1)),
                  pl.BlockSpec((CONV_WIDTH, CONV_CH), lambda bi, c: (0, 0)),
                  pl.BlockSpec((1, CONV_CH), lambda bi, c: (0, 0)),
                  pl.BlockSpec((1, CONV_CH), lambda bi, c: (0, 0)),
                  pl.BlockSpec((1, CONV_CH), lambda bi, c: (0, 0))],
        out_specs=pl.BlockSpec((1, tc, CONV_CH), lambda bi, c: (bi, c, 0)),
        scratch_shapes=[pltpu.VMEM((CONV_PAD + tc, CONV_CH), F32)],
        compiler_params=_cparams("parallel", "arbitrary"),
        name="conformer_conv",
    )(ag, ag, w_dw.astype(F32), vec(b_dw), vec(ln_g), vec(ln_b))


FF_CHUNK = 256


def _post_mixer_kernel(l_ref, r_ref, wo_ref, x_ref, g1_ref, b1_ref, wg_ref, wu_ref, wd_ref, g2_ref, b2_ref, o_ref):
    tm, kl = l_ref.shape
    hm = tm // 2
    halves = (slice(0, hm), slice(hm, tm))
    n_chunks = D_FF // FF_CHUNK
    chunk = lambda n: slice(n * FF_CHUNK, (n + 1) * FF_CHUNK)
    m = [_dot(l_ref[rows, :], wo_ref[0:kl, :]) + _dot(r_ref[rows, :], wo_ref[kl:, :]) for rows in halves]
    x, xb, gate, up = [None, None], [None, None], [None, None], [None, None]
    for p, rows in enumerate(halves):
        x[p] = _layer_norm(ALPHA * x_ref[rows, :] + m[p], g1_ref[...], b1_ref[...])
        xb[p] = x[p].astype(BF16)
        gate[p] = _dot(xb[p], wg_ref[:, chunk(0)])
        up[p] = _dot(xb[p], wu_ref[:, chunk(0)])
    xb = jnp.concatenate(xb, axis=0)
    gate = jnp.concatenate(gate, axis=0)
    up = jnp.concatenate(up, axis=0)
    acc = jnp.zeros((tm, x_ref.shape[1]), F32)
    for n in range(n_chunks):
        if n > 0:
            gate = _dot(xb, wg_ref[:, chunk(n)])
            up = _dot(xb, wu_ref[:, chunk(n)])
        h = (gate * _sigmoid(gate) * up).astype(BF16)
        if n < n_chunks - 1:
            acc = acc + _dot(h, wd_ref[chunk(n), :])
    for p, rows in enumerate(halves):
        y = acc[rows, :] + _dot(h[rows, :], wd_ref[chunk(n_chunks - 1), :])
        o_ref[rows, :] = _layer_norm(ALPHA * x[p] + y, g2_ref[...], b2_ref[...])


def _post_mixer(left, right, wo, x, g1, b1, wg, wu, wd, g2, b2, tm=512):
    m, d = x.shape
    kl, kr = left.shape[1], right.shape[1]
    ff = wg.shape[1]
    row = lambda w: pl.BlockSpec((tm, w), lambda i: (i, 0))
    res = lambda shape: pl.BlockSpec(shape, lambda i: (0, 0), pipeline_mode=pl.Buffered(1))
    vec = lambda v: v.reshape(1, d).astype(F32)
    return pl.pallas_call(
        _post_mixer_kernel,
        out_shape=jax.ShapeDtypeStruct((m, d), F32),
        grid=(m // tm,),
        in_specs=[row(kl), row(kr), res((kl + kr, d)), row(d), res((1, d)), res((1, d)),
                  res((d, ff)), res((d, ff)), res((ff, d)), res((1, d)), res((1, d))],
        out_specs=row(d),
        compiler_params=_cparams("parallel"),
        name="post_mixer",
    )(left, right, wo, x, vec(g1), vec(b1), wg, wu, wd, vec(g2), vec(b2))


def _compress_kernel(xk_ref, xv_ref, pos_ref, w1_ref, w2_ref, kc_ref, vct_ref):
    n = xk_ref.shape[1] // CMP_STRIDE
    out = []
    for s, x_ref in enumerate((xk_ref, xv_ref)):
        xs = [x_ref[0, pl.ds(r, n, stride=CMP_STRIDE), :] for r in range(CMP_STRIDE)]
        half = lambda o: jnp.concatenate([x + pos_ref[s, o + r:o + r + 1, :] for r, x in enumerate(xs)], axis=1)
        top = _dot(half(0).astype(BF16), w1_ref[s, 0])
        bot = _dot(half(CMP_STRIDE).astype(BF16), w1_ref[s, 1])
        h = top + pltpu.roll(bot, n - 1, 0)
        out.append(_dot(_gelu_tanh(h).astype(BF16), w2_ref[s]))
    kc_ref[0] = out[0].astype(kc_ref.dtype)
    vct_ref[0] = out[1].astype(vct_ref.dtype).astype(F32).T.astype(vct_ref.dtype)


def _block_diag2(a):
    z = jnp.zeros_like(a)
    return jnp.concatenate([jnp.concatenate([a, z], axis=-1), jnp.concatenate([z, a], axis=-1)], axis=-2)


def _compress(cv, pos_k, pos_v, w1k, w1v, w2k, w2v):
    b, s, _ = cv.shape
    n = s // CMP_STRIDE
    pos = jnp.stack([jnp.concatenate([p, p], axis=-1) for p in (pos_k, pos_v)]).astype(F32)
    w1 = jnp.stack([_block_diag2(w.reshape(CMP_BLOCK, HEAD_DIM, HEAD_DIM)) for w in (w1k, w1v)])
    w1 = w1.reshape(2, 2, CMP_STRIDE * KV_WIDTH, KV_WIDTH).astype(BF16)
    w2 = jnp.stack([_block_diag2(w) for w in (w2k, w2v)]).astype(BF16)
    full = lambda a: pl.BlockSpec(a.shape, lambda bi: (0,) * a.ndim)
    return pl.pallas_call(
        _compress_kernel,
        out_shape=[jax.ShapeDtypeStruct((b, n, KV_WIDTH), BF16), jax.ShapeDtypeStruct((b, KV_WIDTH, n), BF16)],
        grid=(b,),
        in_specs=[pl.BlockSpec((1, s, KV_WIDTH), lambda bi: (bi, 0, 0)),
                  pl.BlockSpec((1, s, KV_WIDTH), lambda bi: (bi, 0, 1)), full(pos), full(w1), full(w2)],
        out_specs=[pl.BlockSpec((1, n, KV_WIDTH), lambda bi: (bi, 0, 0)),
                   pl.BlockSpec((1, KV_WIDTH, n), lambda bi: (bi, 0, 0))],
        compiler_params=_cparams("parallel"),
        name="nsa_compress",
    )(cv, cv, pos, w1, w2)


NSA_GROUP = 4


def _nsa_kernel(q_ref, ks_ref, vst_ref, kw_ref, vwt_ref, kc_ref, vct_ref, gt_ref, o_ref, *, t, n_cmp):
    i = pl.program_id(1)
    nh = NSA_GROUP
    lane = lax.broadcasted_iota(jnp.int32, (1, LANES), 1)
    tpos = i * t + lax.broadcasted_iota(jnp.int32, (1, t), 1)
    krow = lax.broadcasted_iota(jnp.int32, (t, 1), 0)
    qg = []
    for g in range(2):
        hm = jnp.where(lane // HEAD_DIM == g, 1.0, 0.0).astype(BF16)
        qg.append(jnp.concatenate([q_ref[0, :, m * LANES:(m + 1) * LANES] * hm for m in range(nh)], axis=0))

    nblk = lax.broadcasted_iota(jnp.int32, (LANES, 1), 0)
    cvalid = (nblk * CMP_STRIDE + (CMP_BLOCK - 1) <= tpos) & (nblk < n_cmp)
    jr = lax.broadcasted_iota(jnp.int32, (SEL_LANES, LANES), 0)
    nc = lax.broadcasted_iota(jnp.int32, (SEL_LANES, LANES), 1)
    ovl_t = jnp.where((nc * CMP_STRIDE < jr * SEL_BLOCK + SEL_BLOCK) & (nc * CMP_STRIDE + CMP_BLOCK > jr * SEL_BLOCK)
                      & (nc < n_cmp), 1.0, 0.0).astype(BF16)
    jsel = lax.broadcasted_iota(jnp.int32, (SEL_LANES, 1), 0)
    blk_t = tpos // SEL_BLOCK
    forced = (jsel == 0) | (jsel == blk_t) | (jsel == blk_t - 1)
    pad_rows = lax.broadcasted_iota(jnp.int32, (LANES - SEL_LANES, 1), 0)
    o_cmp = []
    qx = []
    for g in range(2):
        st = _dot_nt(kc_ref[0], qg[g])
        ps = []
        psum = jnp.zeros((LANES, t), F32)
        for m in range(nh):
            s = jnp.where(cvalid, st[:, m * t:(m + 1) * t], NEG)
            e = jnp.where(cvalid, jnp.exp2(s - jnp.max(s, axis=0, keepdims=True)), 0.0)
            l = jnp.sum(e, axis=0, keepdims=True)
            p = e * (1.0 / jnp.where(l > 0.0, l, 1.0))
            ps.append(p.astype(BF16))
            psum = psum + p
        o_cmp.append(_dot(vct_ref[0, g * HEAD_DIM:(g + 1) * HEAD_DIM, :], jnp.concatenate(ps, axis=1)))
        hi = psum.astype(BF16)
        lo = (psum - hi.astype(F32)).astype(BF16)
        it = _dot(ovl_t, jnp.concatenate([hi, lo], axis=1))
        imp = it[:, :t] + it[:, t:]
        score = jnp.where(jsel <= blk_t, imp + jnp.where(forced, FORCE_BONUS, 0.0), NEG)
        cnt = jnp.zeros((SEL_LANES, t), F32)
        for c in range(SEL_LANES):
            row = score[c:c + 1, :]
            cnt = cnt + jnp.where((row > score) | ((row == score) & (jsel > c)), 1.0, 0.0)
        tail = jnp.where(pad_rows == NSA_BIAS_LANE - SEL_LANES, NEG, 0.0) + jnp.zeros((1, t), F32)
        selt = jnp.concatenate([jnp.where(cnt < SEL_TOPK, 0.0, NEG), tail], axis=0).T.astype(BF16)
        qx.append(jnp.concatenate([qg[g], jnp.concatenate([selt] * nh, axis=0)], axis=1))

    def attend(streams, scores=None):
        if scores is None:
            scores = [_dot_nt(k, q) for k, _, q, _, _ in streams]
        sb = [s if bias is None else s + jnp.concatenate([bias] * nh, axis=1)
              for s, (_, _, _, bias, _) in zip(scores, streams)]
        mnew = [jnp.maximum(st[0], jnp.max(s, axis=0, keepdims=True)) for s, (_, _, _, _, st) in zip(sb, streams)]
        p = [jnp.exp2(s - mn) for s, mn in zip(sb, mnew)]
        a = [jnp.exp2(st[0] - mn) for mn, (_, _, _, _, st) in zip(mnew, streams)]
        pv = [_dot(vt, pp.astype(BF16)) for pp, (_, vt, _, _, _) in zip(p, streams)]
        return [(mn, aa * st[1] + jnp.sum(pp, axis=0, keepdims=True), aa * st[2] + o)
                for mn, aa, pp, o, (_, _, _, _, st) in zip(mnew, a, p, pv, streams)]

    init = (jnp.full((1, nh * t), NEG, F32), jnp.zeros((1, nh * t), F32), jnp.zeros((HEAD_DIM, nh * t), F32))
    per_tile = t // SEL_BLOCK
    rows_g = lambda g: slice(g * HEAD_DIM, (g + 1) * HEAD_DIM)

    def slc_streams(j, jc, state, bias):
        onehot = (lane == j * per_tile + krow // SEL_BLOCK) | ((lane == NSA_BIAS_LANE) & (j < 0))
        r0 = pl.multiple_of(jc * t, t)
        kx = jnp.concatenate([ks_ref[0, pl.ds(r0, t), :], jnp.where(onehot, 1.0, 0.0).astype(BF16)], axis=1)
        return [(kx, vst_ref[0, jc, rows_g(g), :], qx[g], bias, state[g]) for g in range(2)]

    slc = [init, init]
    win = [init, init]
    n_win = WINDOW // t + 1
    for d in range(n_win):
        j = i - d
        jc = jnp.maximum(j, 0)
        kpos = j * t + krow
        wbias = jnp.where((kpos >= 0) & (kpos <= tpos) & (tpos - kpos < WINDOW), 0.0, NEG)
        causal = jnp.where(kpos <= tpos, 0.0, NEG) if d == 0 else None
        r0 = pl.multiple_of(jc * t, t)
        sstreams = slc_streams(j, jc, slc, causal)
        wstreams = [(None, vwt_ref[0, jc, rows_g(g), :], None, wbias, win[g]) for g in range(2)]
        kwx = jnp.concatenate([kw_ref[0, pl.ds(r0, t), :], jnp.zeros((t, LANES), BF16)], axis=1)
        both = [_dot_nt(jnp.concatenate([sstreams[g][0], kwx], axis=0), qx[g]) for g in range(2)]
        res = attend(sstreams + wstreams, [both[0][:t], both[1][:t], both[0][t:], both[1][t:]])
        slc, win = res[:2], res[2:]

    def slc_body(it, state):
        j = i - it
        return tuple(attend(slc_streams(j, j, state, None)))

    slc = lax.fori_loop(n_win, i + 1, slc_body, tuple(slc))

    gates = _sigmoid(gt_ref[0]).T
    for m in range(nh):
        halves = []
        for g in range(2):
            c = g * 3 * nh + m * 3
            sl = slice(m * t, (m + 1) * t)
            o_s = slc[g][2][:, sl] * (1.0 / slc[g][1][:, sl])
            o_w = win[g][2][:, sl] * (1.0 / win[g][1][:, sl])
            halves.append(gates[c:c + 1, :] * o_cmp[g][:, sl] + gates[c + 1:c + 2, :] * o_s + gates[c + 2:c + 3, :] * o_w)
        o_ref[0, :, m * LANES:(m + 1) * LANES] = jnp.concatenate(halves, axis=0).T.astype(o_ref.dtype)


def _nsa_attention(qk, vt, kc2, vct, gt, n_cmp):
    b, s, _ = qk.shape
    t = vt.shape[-1]
    assert WINDOW % t == 0 and t % SEL_BLOCK == 0
    nt = s // t
    kspec = lambda col: pl.BlockSpec((1, s, LANES), lambda bi, i: (bi, 0, col))
    vspec = lambda blk: pl.BlockSpec((1, nt, LANES, t), lambda bi, i: (bi, 0, blk, 0))
    cspec = pl.BlockSpec((1, LANES, LANES), lambda bi, i: (bi, 0, 0))
    return pl.pallas_call(
        functools.partial(_nsa_kernel, t=t, n_cmp=n_cmp),
        out_shape=jax.ShapeDtypeStruct((b, s, NSA_WIDTH), BF16),
        grid=(b, nt),
        in_specs=[pl.BlockSpec((1, t, NSA_WIDTH), lambda bi, i: (bi, i, 0)),
                  kspec(4), vspec(0), kspec(5), vspec(1), cspec, cspec,
                  pl.BlockSpec((1, t, LANES), lambda bi, i: (bi, i, 0))],
        out_specs=pl.BlockSpec((1, t, NSA_WIDTH), lambda bi, i: (bi, i, 0)),
        compiler_params=_cparams("parallel", "arbitrary"),
        name="nsa_attention",
    )(qk, qk, vt, qk, vt, kc2, vct, gt)


def _gmlp_kernel(u_ref, v_ref, lg_ref, lb_ref, ws_ref, bs_ref, o_ref, *, tc):
    cg = GMLP_WIDTH // GMLP_GROUPS
    u = _gelu_tanh(u_ref[0])
    v = _layer_norm(_gelu_tanh(v_ref[0]), lg_ref[...], lb_ref[...]).astype(BF16)
    rr = lax.broadcasted_iota(jnp.int32, (GMLP_CHUNK, GMLP_CHUNK), 0)
    cc = lax.broadcasted_iota(jnp.int32, (GMLP_CHUNK, GMLP_CHUNK), 1)
    for g in range(GMLP_GROUPS):
        wm = jnp.where(rr >= cc, ws_ref[g], 0.0).astype(BF16)
        bias = bs_ref[:, g:g + 1]
        for c0 in range(0, tc, GMLP_CHUNK):
            mixed = _dot(wm, v[c0:c0 + GMLP_CHUNK, g * cg:(g + 1) * cg]) + bias
            o_ref[0, c0:c0 + GMLP_CHUNK, g * cg:(g + 1) * cg] = (
                u[c0:c0 + GMLP_CHUNK, g * cg:(g + 1) * cg] * mixed).astype(o_ref.dtype)


def _gmlp(uv, ln_g, ln_b, ws, bs, tc=512):
    b, s, _ = uv.shape
    vec = lambda a: a.reshape(1, GMLP_WIDTH).astype(F32)
    return pl.pallas_call(
        functools.partial(_gmlp_kernel, tc=tc),
        out_shape=jax.ShapeDtypeStruct((b, s, GMLP_WIDTH), BF16),
        grid=(b, s // tc),
        in_specs=[pl.BlockSpec((1, tc, GMLP_WIDTH), lambda bi, c: (bi, c, 0)),
                  pl.BlockSpec((1, tc, GMLP_WIDTH), lambda bi, c: (bi, c, 1)),
                  pl.BlockSpec((1, GMLP_WIDTH), lambda bi, c: (0, 0)),
                  pl.BlockSpec((1, GMLP_WIDTH), lambda bi, c: (0, 0)),
                  pl.BlockSpec((GMLP_GROUPS, GMLP_CHUNK, GMLP_CHUNK), lambda bi, c: (0, 0, 0)),
                  pl.BlockSpec((GMLP_CHUNK, GMLP_GROUPS), lambda bi, c: (0, 0))],
        out_specs=pl.BlockSpec((1, tc, GMLP_WIDTH), lambda bi, c: (bi, c, 0)),
        compiler_params=_cparams("parallel", "parallel"),
        name="gmlp",
    )(uv, uv, vec(ln_g), vec(ln_b), ws.astype(F32), jnp.transpose(bs).astype(F32))


def _head_pair_perm():
    m, g, d = np.meshgrid(np.arange(4), np.arange(2), np.arange(HEAD_DIM), indexing="ij")
    return ((g * 4 + m) * HEAD_DIM + d).reshape(-1)


def _even_layer(x, w_in, conv_w, conv_b, conv_ln_g, conv_ln_b, w_out, post, b, s):
    w_sc = jnp.concatenate([w_in[:, :SB_WIDTH] * (QK_SCALE * LOG2E), w_in[:, SB_WIDTH:]], axis=1)
    w_sc = jnp.concatenate([w_sc[:, :2 * SB_WIDTH], w_sc[:, 3 * SB_WIDTH:], w_sc[:, 2 * SB_WIDTH:3 * SB_WIDTH]], axis=1)
    qk, ag, vt = _proj(x, w_sc.astype(BF16), [(0, 2 * SB_WIDTH), (2 * SB_WIDTH, 2 * CONV_CH), (2 * SB_WIDTH + 2 * CONV_CH, SB_WIDTH)],
                       [BF16, F32, BF16], n_plain=2)
    o_sb = _sb_attention(qk.reshape(b, s, -1), vt.reshape(b, s // KEY_TILE, SB_WIDTH, KEY_TILE))
    o_cv = _conformer_conv(ag.reshape(b, s, -1), conv_w, conv_b, conv_ln_g, conv_ln_b)
    return _post_mixer(o_sb.reshape(b * s, -1), o_cv.reshape(b * s, -1), w_out.astype(BF16), x, *post)


def _odd_layer(x, w_in, cmpk_pos, cmpk_w1, cmpk_w2, cmpv_pos, cmpv_w1, cmpv_w2,
               gmlp_ln_g, gmlp_ln_b, gmlp_ws, gmlp_bs, w_out, post, b, s):
    perm = _head_pair_perm()
    o_gt, o_u = NSA_WIDTH + 6 * KV_WIDTH, NSA_WIDTH + 6 * KV_WIDTH + 24
    w_gt = jnp.pad(w_in[:, o_gt:o_u], ((0, 0), (0, LANES - 24)))
    col = lambda n: w_in[:, NSA_WIDTH + n * KV_WIDTH:NSA_WIDTH + (n + 1) * KV_WIDTH]
    w_re = jnp.concatenate([w_in[:, perm] * (QK_SCALE * LOG2E), col(2), col(4), col(0), col(1), w_in[:, o_u:], w_gt,
                            col(3), col(5)], axis=1)
    qk, cv, uv, gt, vt = _proj(
        x, w_re.astype(BF16), [(0, 768), (768, 256), (1024, 1024), (2048, LANES), (2176, 2 * KV_WIDTH)],
        [BF16, F32, F32, F32, BF16], n_plain=4)
    n_cmp = (s - CMP_BLOCK) // CMP_STRIDE + 1
    kc2, vct = _compress(cv.reshape(b, s, -1), cmpk_pos, cmpv_pos, cmpk_w1, cmpv_w1, cmpk_w2, cmpv_w2)
    o_nsa = _nsa_attention(qk.reshape(b, s, -1), vt.reshape(b, s // KEY_TILE, 2 * KV_WIDTH, KEY_TILE), kc2, vct,
                           gt.reshape(b, s, -1), n_cmp)
    o_mlp = _gmlp(uv.reshape(b, s, -1), gmlp_ln_g, gmlp_ln_b, gmlp_ws, gmlp_bs)
    w_out_re = jnp.concatenate([w_out[:NSA_WIDTH][perm], w_out[NSA_WIDTH:]], axis=0)
    return _post_mixer(o_nsa.reshape(b * s, -1), o_mlp.reshape(b * s, -1), w_out_re.astype(BF16), x, *post)


def kernel(x, ev_w_in, ev_conv_w, ev_conv_b, ev_conv_ln_g, ev_conv_ln_b, ev_w_out, od_w_in, od_cmpk_pos, od_cmpk_w1, od_cmpk_w2, od_cmpv_pos, od_cmpv_w1, od_cmpv_w2, od_gmlp_ln_g, od_gmlp_ln_b, od_gmlp_ws, od_gmlp_bs, od_w_out, ffn_w_gate, ffn_w_up, ffn_w_down, ln1_g, ln1_b, ln2_g, ln2_b):
    b, s, d = x.shape
    assert s // CMP_STRIDE == LANES and s // SEL_BLOCK == SEL_LANES
    h = x.reshape(b * s, d)
    for layer in range(DEPTH):
        i = layer // 2
        post = (ln1_g[layer], ln1_b[layer], ffn_w_gate[layer].astype(BF16), ffn_w_up[layer].astype(BF16),
                ffn_w_down[layer].astype(BF16), ln2_g[layer], ln2_b[layer])
        if layer % 2 == 0:
            h = _even_layer(h, ev_w_in[i], ev_conv_w[i], ev_conv_b[i], ev_conv_ln_g[i], ev_conv_ln_b[i],
                            ev_w_out[i], post, b, s)
        else:
            h = _odd_layer(h, od_w_in[i], od_cmpk_pos[i], od_cmpk_w1[i], od_cmpk_w2[i],
                           od_cmpv_pos[i], od_cmpv_w1[i], od_cmpv_w2[i], od_gmlp_ln_g[i], od_gmlp_ln_b[i],
                           od_gmlp_ws[i], od_gmlp_bs[i], od_w_out[i], post, b, s)
    return h.reshape(b, s, d)
```

```python
import functools

import numpy as np
import jax
import jax.numpy as jnp
from jax import lax
from jax.experimental import pallas as pl
from jax.experimental.pallas import tpu as pltpu

F32 = jnp.float32
BF16 = jnp.bfloat16

D_MODEL = 1024
DEPTH = 2
HEAD_DIM = 64
LANES = 128
SUBLANES = 8
SB_WIDTH = 512
SB_HEADS = 8
CONV_CH = 512
CONV_WIDTH = 31
NSA_WIDTH = 512
KV_WIDTH = 128
CMP_BLOCK = 32
CMP_STRIDE = 16
SEL_BLOCK = 64
SEL_TOPK = 8
SEL_LANES = 32
WINDOW = 512
GMLP_WIDTH = 512
GMLP_GROUPS = 4
GMLP_CHUNK = 128
D_FF = 2816
ALPHA = (2 * DEPTH) ** 0.25
LN_EPS = 1e-5
NEG = -1e30
FORCE_BONUS = 1e3
QK_SCALE = HEAD_DIM ** -0.5
LOG2E = float(np.log2(np.e))
NSA_BIAS_LANE = SEL_LANES

VMEM_LIMIT = 56 * 1024 * 1024


def _cparams(*sem):
    return pltpu.CompilerParams(dimension_semantics=sem, vmem_limit_bytes=VMEM_LIMIT)


def _layer_norm(y, g, b):
    mu = jnp.mean(y, axis=-1, keepdims=True)
    d = y - mu
    var = jnp.mean(d * d, axis=-1, keepdims=True)
    return d * lax.rsqrt(var + LN_EPS) * g + b


def _sigmoid(x):
    return 1.0 / (1.0 + jnp.exp(-x))


def _gelu_tanh(x):
    c = np.float32(np.sqrt(2.0 / np.pi))
    return x * (0.5 * (1.0 + jnp.tanh(c * (x + 0.044715 * (x * x * x)))))


def _dot(a, b):
    return jnp.dot(a, b, preferred_element_type=F32)


def _dot_nt(a, b):
    return lax.dot_general(a, b, (((1,), (1,)), ((), ())), preferred_element_type=F32)


KEY_TILE = 256


def _proj_kernel(x_ref, w_ref, *out_refs, splits, n_plain):
    xb = x_ref[...].astype(BF16)
    for n, ((start, width), o_ref) in enumerate(zip(splits, out_refs)):
        y = _dot(xb, w_ref[:, start:start + width]).astype(o_ref.dtype)
        if n < n_plain:
            o_ref[...] = y
        else:
            for s in range(o_ref.shape[0]):
                o_ref[s] = y[s * KEY_TILE:(s + 1) * KEY_TILE, :].astype(F32).T.astype(o_ref.dtype)


def _proj(x, w, splits, dtypes, n_plain, tm=1024):
    m, k = x.shape
    n = w.shape[1]
    plain, tposed = splits[:n_plain], splits[n_plain:]
    return pl.pallas_call(
        functools.partial(_proj_kernel, splits=tuple(splits), n_plain=n_plain),
        out_shape=[jax.ShapeDtypeStruct((m, wd), dt) for (_, wd), dt in zip(plain, dtypes)]
        + [jax.ShapeDtypeStruct((m // KEY_TILE, wd, KEY_TILE), dt) for (_, wd), dt in zip(tposed, dtypes[n_plain:])],
        grid=(m // tm,),
        in_specs=[pl.BlockSpec((tm, k), lambda i: (i, 0)),
                  pl.BlockSpec((k, n), lambda i: (0, 0))],
        out_specs=[pl.BlockSpec((tm, wd), lambda i: (i, 0)) for (_, wd) in plain]
        + [pl.BlockSpec((tm // KEY_TILE, wd, KEY_TILE), lambda i: (i, 0, 0)) for (_, wd) in tposed],
        compiler_params=_cparams("parallel"),
        name="proj_in",
    )(x, w)


SB_UNDERFLOW = 150.0


def _sb_kernel(q_ref, k_ref, vt_ref, o_ref, *, t, nh):
    i = pl.program_id(2)
    lane = lax.broadcasted_iota(jnp.int32, (1, LANES), 1)
    rr = lax.broadcasted_iota(jnp.int32, (t, t), 0)
    cc = lax.broadcasted_iota(jnp.int32, (t, t), 1)
    before = rr < cc
    later = jnp.where(before, 1.0, 0.0).astype(BF16)
    hs = range(nh)
    pair = lambda h: slice((h // 2) * LANES, (h // 2 + 1) * LANES)
    qh = [q_ref[0, :, pair(h)] * jnp.where(lane // HEAD_DIM == h % 2, 1.0, 0.0).astype(BF16) for h in hs]

    def tile(j, state, diag):
        r0 = pl.multiple_of(j * t, t)
        z = [_dot_nt(k_ref[0, pl.ds(r0, t), pair(h)], qh[h]) for h in hs]
        nz = [-z[h] for h in hs]
        lg = [jnp.log2(1.0 + jnp.exp2(jnp.minimum(z[h], nz[h]))) for h in hs]
        lb = [jnp.minimum(nz[h], 0.0) - lg[h] for h in hs]
        lr = [jnp.where(before, lb[h], 0.0) for h in hs] if diag else lb
        hi = [lr[h].astype(BF16) for h in hs]
        hl = [jnp.concatenate([hi[h], (lr[h] - hi[h].astype(F32)).astype(BF16)], axis=1) for h in hs]
        bt = [_dot(later, hl[h]) for h in hs]
        w = [jnp.exp2(lb[h] + z[h] + (bt[h][:, :t] + bt[h][:, t:] + state[h][0])) for h in hs]
        if diag:
            w = [jnp.where(before, w[h], 0.0) for h in hs]
        pv = [_dot(vt_ref[0, j, h * HEAD_DIM:(h + 1) * HEAD_DIM, :], w[h].astype(BF16)) for h in hs]
        return tuple((state[h][0] + jnp.sum(lr[h], axis=0, keepdims=True), state[h][1] + pv[h]) for h in hs)

    zero = (jnp.zeros((1, t), F32), jnp.zeros((HEAD_DIM, t), F32))
    state = tile(i, (zero,) * nh, True)

    def cond(c):
        return (c[0] >= 0) & (c[1] > -SB_UNDERFLOW)

    def body(c):
        st = tile(c[0], c[2], False)
        top = st[0][0]
        for h in range(1, nh):
            top = jnp.maximum(top, st[h][0])
        return c[0] - 1, jnp.max(top), st

    _, _, state = lax.while_loop(cond, body, (i - 1, jnp.float32(0.0), state))
    for p in range(nh // 2):
        o_ref[0, :, p * LANES:(p + 1) * LANES] = jnp.concatenate(
            [state[2 * p][1], state[2 * p + 1][1]], axis=0).T.astype(o_ref.dtype)


def _sb_attention(qk, vt, nh=SB_HEADS):
    b, s, _ = qk.shape
    t = vt.shape[-1]
    wb = nh * HEAD_DIM
    nblk = SB_WIDTH // wb
    return pl.pallas_call(
        functools.partial(_sb_kernel, t=t, nh=nh),
        out_shape=jax.ShapeDtypeStruct((b, s, SB_WIDTH), BF16),
        grid=(b, nblk, s // t),
        in_specs=[pl.BlockSpec((1, t, wb), lambda bi, hb, i: (bi, i, hb)),
                  pl.BlockSpec((1, s, wb), lambda bi, hb, i: (bi, 0, nblk + hb)),
                  pl.BlockSpec((1, s // t, wb, t), lambda bi, hb, i: (bi, 0, hb, 0))],
        out_specs=pl.BlockSpec((1, t, wb), lambda bi, hb, i: (bi, i, hb)),
        compiler_params=_cparams("parallel", "parallel", "arbitrary"),
        name="sb_attention",
    )(qk, qk, vt)


CONV_PAD = 32


def _conv_kernel(a_ref, g_ref, w_ref, b_ref, lg_ref, lb_ref, o_ref, hbuf, *, tc, sub):
    c = pl.program_id(1)

    @pl.when(c == 0)
    def _():
        hbuf[0:CONV_PAD, :] = jnp.zeros((CONV_PAD, CONV_CH), F32)

    hbuf[CONV_PAD:CONV_PAD + tc, :] = a_ref[0] * _sigmoid(g_ref[0])
    off = CONV_PAD - (CONV_WIDTH - 1)
    for r0 in range(0, tc, sub):
        acc = jnp.zeros((sub, CONV_CH), F32) + b_ref[...]
        for ph in range(SUBLANES):
            taps = [w for w in range(CONV_WIDTH) if (off + w) % SUBLANES == ph]
            rows = sub + (SUBLANES if ph else 0)
            part = None
            for w in taps:
                a0 = r0 + (off + w) // SUBLANES * SUBLANES
                term = w_ref[w:w + 1, :] * hbuf[a0:a0 + rows, :]
                part = term if part is None else part + term
            acc = acc + part[ph:ph + sub, :]
        y = _layer_norm(acc, lg_ref[...], lb_ref[...])
        o_ref[0, r0:r0 + sub, :] = (y * _sigmoid(y)).astype(o_ref.dtype)
    hbuf[0:CONV_PAD, :] = hbuf[tc:tc + CONV_PAD, :]


def _conformer_conv(ag, w_dw, b_dw, ln_g, ln_b, tc=512, sub=128):
    b, s, _ = ag.shape
    vec = lambda v: v.reshape(1, CONV_CH).astype(F32)
    return pl.pallas_call(
        functools.partial(_conv_kernel, tc=tc, sub=sub),
        out_shape=jax.ShapeDtypeStruct((b, s, CONV_CH), BF16),
        grid=(b, s // tc),
        in_specs=[pl.BlockSpec((1, tc, CONV_CH), lambda bi, c: (bi, c, 0)),
                  pl.BlockSpec((1, tc, CONV_CH), lambda bi, c: (bi, c, 1)),
                  pl.BlockSpec((CONV_WIDTH, CONV_CH), lambda bi, c: (0, 0)),
                  pl.BlockSpec((1, CONV_CH), lambda bi, c: (0, 0)),
                  pl.BlockSpec((1, CONV_CH), lambda bi, c: (0, 0)),
                  pl.BlockSpec((1, CONV_CH), lambda bi, c: (0, 0))],
        out_specs=pl.BlockSpec((1, tc, CONV_CH), lambda bi, c: (bi, c, 0)),
        scratch_shapes=[pltpu.VMEM((CONV_PAD + tc, CONV_CH), F32)],
        compiler_params=_cparams("parallel", "arbitrary"),
        name="conformer_conv",
    )(ag, ag, w_dw.astype(F32), vec(b_dw), vec(ln_g), vec(ln_b))


FF_CHUNK = 256


def _post_mixer_kernel(l_ref, r_ref, wo_ref, x_ref, g1_ref, b1_ref, wg_ref, wu_ref, wd_ref, g2_ref, b2_ref, o_ref):
    tm, kl = l_ref.shape
    hm = tm // 2
    halves = (slice(0, hm), slice(hm, tm))
    n_chunks = D_FF // FF_CHUNK
    chunk = lambda n: slice(n * FF_CHUNK, (n + 1) * FF_CHUNK)
    m = [_dot(l_ref[rows, :], wo_ref[0:kl, :]) + _dot(r_ref[rows, :], wo_ref[kl:, :]) for rows in halves]
    x, xb, gate, up = [None, None], [None, None], [None, None], [None, None]
    for p, rows in enumerate(halves):
        x[p] = _layer_norm(ALPHA * x_ref[rows, :] + m[p], g1_ref[...], b1_ref[...])
        xb[p] = x[p].astype(BF16)
        gate[p] = _dot(xb[p], wg_ref[:, chunk(0)])
        up[p] = _dot(xb[p], wu_ref[:, chunk(0)])
    xb = jnp.concatenate(xb, axis=0)
    gate = jnp.concatenate(gate, axis=0)
    up = jnp.concatenate(up, axis=0)
    acc = jnp.zeros((tm, x_ref.shape[1]), F32)
    for n in range(n_chunks):
        if n > 0:
            gate = _dot(xb, wg_ref[:, chunk(n)])
            up = _dot(xb, wu_ref[:, chunk(n)])
        h = (gate * _sigmoid(gate) * up).astype(BF16)
        if n < n_chunks - 1:
            acc = acc + _dot(h, wd_ref[chunk(n), :])
    for p, rows in enumerate(halves):
        y = acc[rows, :] + _dot(h[rows, :], wd_ref[chunk(n_chunks - 1), :])
        o_ref[rows, :] = _layer_norm(ALPHA * x[p] + y, g2_ref[...], b2_ref[...])


def _post_mixer(left, right, wo, x, g1, b1, wg, wu, wd, g2, b2, tm=512):
    m, d = x.shape
    kl, kr = left.shape[1], right.shape[1]
    ff = wg.shape[1]
    row = lambda w: pl.BlockSpec((tm, w), lambda i: (i, 0))
    res = lambda shape: pl.BlockSpec(shape, lambda i: (0, 0), pipeline_mode=pl.Buffered(1))
    vec = lambda v: v.reshape(1, d).astype(F32)
    return pl.pallas_call(
        _post_mixer_kernel,
        out_shape=jax.ShapeDtypeStruct((m, d), F32),
        grid=(m // tm,),
        in_specs=[row(kl), row(kr), res((kl + kr, d)), row(d), res((1, d)), res((1, d)),
                  res((d, ff)), res((d, ff)), res((ff, d)), res((1, d)), res((1, d))],
        out_specs=row(d),
        compiler_params=_cparams("parallel"),
        name="post_mixer",
    )(left, right, wo, x, vec(g1), vec(b1), wg, wu, wd, vec(g2), vec(b2))


def _compress_kernel(xk_ref, xv_ref, pos_ref, w1_ref, w2_ref, kc_ref, vct_ref):
    n = xk_ref.shape[1] // CMP_STRIDE
    out = []
    for s, x_ref in enumerate((xk_ref, xv_ref)):
        xs = [x_ref[0, pl.ds(r, n, stride=CMP_STRIDE), :] for r in range(CMP_STRIDE)]
        half = lambda o: jnp.concatenate([x + pos_ref[s, o + r:o + r + 1, :] for r, x in enumerate(xs)], axis=1)
        top = _dot(half(0).astype(BF16), w1_ref[s, 0])
        bot = _dot(half(CMP_STRIDE).astype(BF16), w1_ref[s, 1])
        h = top + pltpu.roll(bot, n - 1, 0)
        out.append(_dot(_gelu_tanh(h).astype(BF16), w2_ref[s]))
    kc_ref[0] = out[0].astype(kc_ref.dtype)
    vct_ref[0] = out[1].astype(vct_ref.dtype).astype(F32).T.astype(vct_ref.dtype)


def _block_diag2(a):
    z = jnp.zeros_like(a)
    return jnp.concatenate([jnp.concatenate([a, z], axis=-1), jnp.concatenate([z, a], axis=-1)], axis=-2)


def _compress(cv, pos_k, pos_v, w1k, w1v, w2k, w2v):
    b, s, _ = cv.shape
    n = s // CMP_STRIDE
    pos = jnp.stack([jnp.concatenate([p, p], axis=-1) for p in (pos_k, pos_v)]).astype(F32)
    w1 = jnp.stack([_block_diag2(w.reshape(CMP_BLOCK, HEAD_DIM, HEAD_DIM)) for w in (w1k, w1v)])
    w1 = w1.reshape(2, 2, CMP_STRIDE * KV_WIDTH, KV_WIDTH).astype(BF16)
    w2 = jnp.stack([_block_diag2(w) for w in (w2k, w2v)]).astype(BF16)
    full = lambda a: pl.BlockSpec(a.shape, lambda bi: (0,) * a.ndim)
    return pl.pallas_call(
        _compress_kernel,
        out_shape=[jax.ShapeDtypeStruct((b, n, KV_WIDTH), BF16), jax.ShapeDtypeStruct((b, KV_WIDTH, n), BF16)],
        grid=(b,),
        in_specs=[pl.BlockSpec((1, s, KV_WIDTH), lambda bi: (bi, 0, 0)),
                  pl.BlockSpec((1, s, KV_WIDTH), lambda bi: (bi, 0, 1)), full(pos), full(w1), full(w2)],
        out_specs=[pl.BlockSpec((1, n, KV_WIDTH), lambda bi: (bi, 0, 0)),
                   pl.BlockSpec((1, KV_WIDTH, n), lambda bi: (bi, 0, 0))],
        compiler_params=_cparams("parallel"),
        name="nsa_compress",
    )(cv, cv, pos, w1, w2)


NSA_GROUP = 4


def _nsa_kernel(q_ref, ks_ref, vst_ref, kw_ref, vwt_ref, kc_ref, vct_ref, gt_ref, o_ref, *, t, n_cmp):
    i = pl.program_id(1)
    nh = NSA_GROUP
    lane = lax.broadcasted_iota(jnp.int32, (1, LANES), 1)
    tpos = i * t + lax.broadcasted_iota(jnp.int32, (1, t), 1)
    krow = lax.broadcasted_iota(jnp.int32, (t, 1), 0)
    qg = []
    for g in range(2):
        hm = jnp.where(lane // HEAD_DIM == g, 1.0, 0.0).astype(BF16)
        qg.append(jnp.concatenate([q_ref[0, :, m * LANES:(m + 1) * LANES] * hm for m in range(nh)], axis=0))

    nblk = lax.broadcasted_iota(jnp.int32, (LANES, 1), 0)
    cvalid = (nblk * CMP_STRIDE + (CMP_BLOCK - 1) <= tpos) & (nblk < n_cmp)
    jr = lax.broadcasted_iota(jnp.int32, (SEL_LANES, LANES), 0)
    nc = lax.broadcasted_iota(jnp.int32, (SEL_LANES, LANES), 1)
    ovl_t = jnp.where((nc * CMP_STRIDE < jr * SEL_BLOCK + SEL_BLOCK) & (nc * CMP_STRIDE + CMP_BLOCK > jr * SEL_BLOCK)
                      & (nc < n_cmp), 1.0, 0.0).astype(BF16)
    jsel = lax.broadcasted_iota(jnp.int32, (SEL_LANES, 1), 0)
    blk_t = tpos // SEL_BLOCK
    forced = (jsel == 0) | (jsel == blk_t) | (jsel == blk_t - 1)
    pad_rows = lax.broadcasted_iota(jnp.int32, (LANES - SEL_LANES, 1), 0)
    o_cmp = []
    qx = []
    for g in range(2):
        st = _dot_nt(kc_ref[0], qg[g])
        ps = []
        psum = jnp.zeros((LANES, t), F32)
        for m in range(nh):
            s = jnp.where(cvalid, st[:, m * t:(m + 1) * t], NEG)
            e = jnp.where(cvalid, jnp.exp2(s - jnp.max(s, axis=0, keepdims=True)), 0.0)
            l = jnp.sum(e, axis=0, keepdims=True)
            p = e * (1.0 / jnp.where(l > 0.0, l, 1.0))
            ps.append(p.astype(BF16))
            psum = psum + p
        o_cmp.append(_dot(vct_ref[0, g * HEAD_DIM:(g + 1) * HEAD_DIM, :], jnp.concatenate(ps, axis=1)))
        hi = psum.astype(BF16)
        lo = (psum - hi.astype(F32)).astype(BF16)
        it = _dot(ovl_t, jnp.concatenate([hi, lo], axis=1))
        imp = it[:, :t] + it[:, t:]
        score = jnp.where(jsel <= blk_t, imp + jnp.where(forced, FORCE_BONUS, 0.0), NEG)
        cnt = jnp.zeros((SEL_LANES, t), F32)
        for c in range(SEL_LANES):
            row = score[c:c + 1, :]
            cnt = cnt + jnp.where((row > score) | ((row == score) & (jsel > c)), 1.0, 0.0)
        tail = jnp.where(pad_rows == NSA_BIAS_LANE - SEL_LANES, NEG, 0.0) + jnp.zeros((1, t), F32)
        selt = jnp.concatenate([jnp.where(cnt < SEL_TOPK, 0.0, NEG), tail], axis=0).T.astype(BF16)
        qx.append(jnp.concatenate([qg[g], jnp.concatenate([selt] * nh, axis=0)], axis=1))

    def attend(streams, scores=None):
        if scores is None:
            scores = [_dot_nt(k, q) for k, _, q, _, _ in streams]
        sb = [s if bias is None else s + jnp.concatenate([bias] * nh, axis=1)
              for s, (_, _, _, bias, _) in zip(scores, streams)]
        mnew = [jnp.maximum(st[0], jnp.max(s, axis=0, keepdims=True)) for s, (_, _, _, _, st) in zip(sb, streams)]
        p = [jnp.exp2(s - mn) for s, mn in zip(sb, mnew)]
        a = [jnp.exp2(st[0] - mn) for mn, (_, _, _, _, st) in zip(mnew, streams)]
        pv = [_dot(vt, pp.astype(BF16)) for pp, (_, vt, _, _, _) in zip(p, streams)]
        return [(mn, aa * st[1] + jnp.sum(pp, axis=0, keepdims=True), aa * st[2] + o)
                for mn, aa, pp, o, (_, _, _, _, st) in zip(mnew, a, p, pv, streams)]

    init = (jnp.full((1, nh * t), NEG, F32), jnp.zeros((1, nh * t), F32), jnp.zeros((HEAD_DIM, nh * t), F32))
    per_tile = t // SEL_BLOCK
    rows_g = lambda g: slice(g * HEAD_DIM, (g + 1) * HEAD_DIM)

    def slc_streams(j, jc, state, bias):
        onehot = (lane == j * per_tile + krow // SEL_BLOCK) | ((lane == NSA_BIAS_LANE) & (j < 0))
        r0 = pl.multiple_of(jc * t, t)
        kx = jnp.concatenate([ks_ref[0, pl.ds(r0, t), :], jnp.where(onehot, 1.0, 0.0).astype(BF16)], axis=1)
        return [(kx, vst_ref[0, jc, rows_g(g), :], qx[g], bias, state[g]) for g in range(2)]

    slc = [init, init]
    win = [init, init]
    n_win = WINDOW // t + 1
    for d in range(n_win):
        j = i - d
        jc = jnp.maximum(j, 0)
        kpos = j * t + krow
        wbias = jnp.where((kpos >= 0) & (kpos <= tpos) & (tpos - kpos < WINDOW), 0.0, NEG)
        causal = jnp.where(kpos <= tpos, 0.0, NEG) if d == 0 else None
        r0 = pl.multiple_of(jc * t, t)
        sstreams = slc_streams(j, jc, slc, causal)
        wstreams = [(None, vwt_ref[0, jc, rows_g(g), :], None, wbias, win[g]) for g in range(2)]
        kwx = jnp.concatenate([kw_ref[0, pl.ds(r0, t), :], jnp.zeros((t, LANES), BF16)], axis=1)
        both = [_dot_nt(jnp.concatenate([sstreams[g][0], kwx], axis=0), qx[g]) for g in range(2)]
        res = attend(sstreams + wstreams, [both[0][:t], both[1][:t], both[0][t:], both[1][t:]])
        slc, win = res[:2], res[2:]

    def slc_body(it, state):
        j = i - it
        return tuple(attend(slc_streams(j, j, state, None)))

    slc = lax.fori_loop(n_win, i + 1, slc_body, tuple(slc))

    gates = _sigmoid(gt_ref[0]).T
    for m in range(nh):
        halves = []
        for g in range(2):
            c = g * 3 * nh + m * 3
            sl = slice(m * t, (m + 1) * t)
            o_s = slc[g][2][:, sl] * (1.0 / slc[g][1][:, sl])
            o_w = win[g][2][:, sl] * (1.0 / win[g][1][:, sl])
            halves.append(gates[c:c + 1, :] * o_cmp[g][:, sl] + gates[c + 1:c + 2, :] * o_s + gates[c + 2:c + 3, :] * o_w)
        o_ref[0, :, m * LANES:(m + 1) * LANES] = jnp.concatenate(halves, axis=0).T.astype(o_ref.dtype)


def _nsa_attention(qk, vt, kc2, vct, gt, n_cmp):
    b, s, _ = qk.shape
    t = vt.shape[-1]
    assert WINDOW % t == 0 and t % SEL_BLOCK == 0
    nt = s // t
    kspec = lambda col: pl.BlockSpec((1, s, LANES), lambda bi, i: (bi, 0, col))
    vspec = lambda blk: pl.BlockSpec((1, nt, LANES, t), lambda bi, i: (bi, 0, blk, 0))
    cspec = pl.BlockSpec((1, LANES, LANES), lambda bi, i: (bi, 0, 0))
    return pl.pallas_call(
        functools.partial(_nsa_kernel, t=t, n_cmp=n_cmp),
        out_shape=jax.ShapeDtypeStruct((b, s, NSA_WIDTH), BF16),
        grid=(b, nt),
        in_specs=[pl.BlockSpec((1, t, NSA_WIDTH), lambda bi, i: (bi, i, 0)),
                  kspec(4), vspec(0), kspec(5), vspec(1), cspec, cspec,
                  pl.BlockSpec((1, t, LANES), lambda bi, i: (bi, i, 0))],
        out_specs=pl.BlockSpec((1, t, NSA_WIDTH), lambda bi, i: (bi, i, 0)),
        compiler_params=_cparams("parallel", "arbitrary"),
        name="nsa_attention",
    )(qk, qk, vt, qk, vt, kc2, vct, gt)


def _gmlp_kernel(u_ref, v_ref, lg_ref, lb_ref, ws_ref, bs_ref, o_ref, *, tc):
    cg = GMLP_WIDTH // GMLP_GROUPS
    u = _gelu_tanh(u_ref[0])
    v = _layer_norm(_gelu_tanh(v_ref[0]), lg_ref[...], lb_ref[...]).astype(BF16)
    rr = lax.broadcasted_iota(jnp.int32, (GMLP_CHUNK, GMLP_CHUNK), 0)
    cc = lax.broadcasted_iota(jnp.int32, (GMLP_CHUNK, GMLP_CHUNK), 1)
    for g in range(GMLP_GROUPS):
        wm = jnp.where(rr >= cc, ws_ref[g], 0.0).astype(BF16)
        bias = bs_ref[:, g:g + 1]
        for c0 in range(0, tc, GMLP_CHUNK):
            mixed = _dot(wm, v[c0:c0 + GMLP_CHUNK, g * cg:(g + 1) * cg]) + bias
            o_ref[0, c0:c0 + GMLP_CHUNK, g * cg:(g + 1) * cg] = (
                u[c0:c0 + GMLP_CHUNK, g * cg:(g + 1) * cg] * mixed).astype(o_ref.dtype)


def _gmlp(uv, ln_g, ln_b, ws, bs, tc=1024):
    b, s, _ = uv.shape
    vec = lambda a: a.reshape(1, GMLP_WIDTH).astype(F32)
    return pl.pallas_call(
        functools.partial(_gmlp_kernel, tc=tc),
        out_shape=jax.ShapeDtypeStruct((b, s, GMLP_WIDTH), BF16),
        grid=(b, s // tc),
        in_specs=[pl.BlockSpec((1, tc, GMLP_WIDTH), lambda bi, c: (bi, c, 0)),
                  pl.BlockSpec((1, tc, GMLP_WIDTH), lambda bi, c: (bi, c, 1)),
                  pl.BlockSpec((1, GMLP_WIDTH), lambda bi, c: (0, 0)),
                  pl.BlockSpec((1, GMLP_WIDTH), lambda bi, c: (0, 0)),
                  pl.BlockSpec((GMLP_GROUPS, GMLP_CHUNK, GMLP_CHUNK), lambda bi, c: (0, 0, 0)),
                  pl.BlockSpec((GMLP_CHUNK, GMLP_GROUPS), lambda bi, c: (0, 0))],
        out_specs=pl.BlockSpec((1, tc, GMLP_WIDTH), lambda bi, c: (bi, c, 0)),
        compiler_params=_cparams("parallel", "parallel"),
        name="gmlp",
    )(uv, uv, vec(ln_g), vec(ln_b), ws.astype(F32), jnp.transpose(bs).astype(F32))


def _head_pair_perm():
    m, g, d = np.meshgrid(np.arange(4), np.arange(2), np.arange(HEAD_DIM), indexing="ij")
    return ((g * 4 + m) * HEAD_DIM + d).reshape(-1)


def _even_layer(x, w_in, conv_w, conv_b, conv_ln_g, conv_ln_b, w_out, post, b, s):
    w_sc = jnp.concatenate([w_in[:, :SB_WIDTH] * (QK_SCALE * LOG2E), w_in[:, SB_WIDTH:]], axis=1)
    w_sc = jnp.concatenate([w_sc[:, :2 * SB_WIDTH], w_sc[:, 3 * SB_WIDTH:], w_sc[:, 2 * SB_WIDTH:3 * SB_WIDTH]], axis=1)
    qk, ag, vt = _proj(x, w_sc.astype(BF16), [(0, 2 * SB_WIDTH), (2 * SB_WIDTH, 2 * CONV_CH), (2 * SB_WIDTH + 2 * CONV_CH, SB_WIDTH)],
                       [BF16, F32, BF16], n_plain=2)
    o_sb = _sb_attention(qk.reshape(b, s, -1), vt.reshape(b, s // KEY_TILE, SB_WIDTH, KEY_TILE))
    o_cv = _conformer_conv(ag.reshape(b, s, -1), conv_w, conv_b, conv_ln_g, conv_ln_b)
    return _post_mixer(o_sb.reshape(b * s, -1), o_cv.reshape(b * s, -1), w_out.astype(BF16), x, *post)


def _odd_layer(x, w_in, cmpk_pos, cmpk_w1, cmpk_w2, cmpv_pos, cmpv_w1, cmpv_w2,
               gmlp_ln_g, gmlp_ln_b, gmlp_ws, gmlp_bs, w_out, post, b, s):
    perm = _head_pair_perm()
    o_gt, o_u = NSA_WIDTH + 6 * KV_WIDTH, NSA_WIDTH + 6 * KV_WIDTH + 24
    w_gt = jnp.pad(w_in[:, o_gt:o_u], ((0, 0), (0, LANES - 24)))
    col = lambda n: w_in[:, NSA_WIDTH + n * KV_WIDTH:NSA_WIDTH + (n + 1) * KV_WIDTH]
    w_re = jnp.concatenate([w_in[:, perm] * (QK_SCALE * LOG2E), col(2), col(4), col(0), col(1), w_in[:, o_u:], w_gt,
                            col(3), col(5)], axis=1)
    qk, cv, uv, gt, vt = _proj(
        x, w_re.astype(BF16), [(0, 768), (768, 256), (1024, 1024), (2048, LANES), (2176, 2 * KV_WIDTH)],
        [BF16, F32, F32, F32, BF16], n_plain=4)
    n_cmp = (s - CMP_BLOCK) // CMP_STRIDE + 1
    kc2, vct = _compress(cv.reshape(b, s, -1), cmpk_pos, cmpv_pos, cmpk_w1, cmpv_w1, cmpk_w2, cmpv_w2)
    o_nsa = _nsa_attention(qk.reshape(b, s, -1), vt.reshape(b, s // KEY_TILE, 2 * KV_WIDTH, KEY_TILE), kc2, vct,
                           gt.reshape(b, s, -1), n_cmp)
    o_mlp = _gmlp(uv.reshape(b, s, -1), gmlp_ln_g, gmlp_ln_b, gmlp_ws, gmlp_bs)
    w_out_re = jnp.concatenate([w_out[:NSA_WIDTH][perm], w_out[NSA_WIDTH:]], axis=0)
    return _post_mixer(o_nsa.reshape(b * s, -1), o_mlp.reshape(b * s, -1), w_out_re.astype(BF16), x, *post)


def kernel(x, ev_w_in, ev_conv_w, ev_conv_b, ev_conv_ln_g, ev_conv_ln_b, ev_w_out, od_w_in, od_cmpk_pos, od_cmpk_w1, od_cmpk_w2, od_cmpv_pos, od_cmpv_w1, od_cmpv_w2, od_gmlp_ln_g, od_gmlp_ln_b, od_gmlp_ws, od_gmlp_bs, od_w_out, ffn_w_gate, ffn_w_up, ffn_w_down, ln1_g, ln1_b, ln2_g, ln2_b):
    b, s, d = x.shape
    assert s // CMP_STRIDE == LANES and s // SEL_BLOCK == SEL_LANES
    h = x.reshape(b * s, d)
    for layer in range(DEPTH):
        i = layer // 2
        post = (ln1_g[layer], ln1_b[layer], ffn_w_gate[layer].astype(BF16), ffn_w_up[layer].astype(BF16),
                ffn_w_down[layer].astype(BF16), ln2_g[layer], ln2_b[layer])
        if layer % 2 == 0:
            h = _even_layer(h, ev_w_in[i], ev_conv_w[i], ev_conv_b[i], ev_conv_ln_g[i], ev_conv_ln_b[i],
                            ev_w_out[i], post, b, s)
        else:
            h = _odd_layer(h, od_w_in[i], od_cmpk_pos[i], od_cmpk_w1[i], od_cmpk_w2[i],
                           od_cmpv_pos[i], od_cmpv_w1[i], od_cmpv_w2[i], od_gmlp_ln_g[i], od_gmlp_ln_b[i],
                           od_gmlp_ws[i], od_gmlp_bs[i], od_w_out[i], post, b, s)
    return h.reshape(b, s, d)
```

```python
import functools

import numpy as np
import jax
import jax.numpy as jnp
from jax import lax
from jax.experimental import pallas as pl
from jax.experimental.pallas import tpu as pltpu

F32 = jnp.float32
BF16 = jnp.bfloat16

D_MODEL = 1024
DEPTH = 2
HEAD_DIM = 64
LANES = 128
SUBLANES = 8
SB_WIDTH = 512
SB_HEADS = 8
CONV_CH = 512
CONV_WIDTH = 31
NSA_WIDTH = 512
KV_WIDTH = 128
CMP_BLOCK = 32
CMP_STRIDE = 16
SEL_BLOCK = 64
SEL_TOPK = 8
SEL_LANES = 32
WINDOW = 512
GMLP_WIDTH = 512
GMLP_GROUPS = 4
GMLP_CHUNK = 128
D_FF = 2816
ALPHA = (2 * DEPTH) ** 0.25
LN_EPS = 1e-5
NEG = -1e30
FORCE_BONUS = 1e3
QK_SCALE = HEAD_DIM ** -0.5
LOG2E = float(np.log2(np.e))
NSA_BIAS_LANE = SEL_LANES

VMEM_LIMIT = 56 * 1024 * 1024


def _cparams(*sem):
    return pltpu.CompilerParams(dimension_semantics=sem, vmem_limit_bytes=VMEM_LIMIT)


def _layer_norm(y, g, b):
    mu = jnp.mean(y, axis=-1, keepdims=True)
    d = y - mu
    var = jnp.mean(d * d, axis=-1, keepdims=True)
    return d * lax.rsqrt(var + LN_EPS) * g + b


def _sigmoid(x):
    return 1.0 / (1.0 + jnp.exp2(x * -LOG2E))


def _gelu_tanh(x):
    c = np.float32(np.sqrt(2.0 / np.pi))
    return x * (0.5 * (1.0 + jnp.tanh(c * (x + 0.044715 * (x * x * x)))))


def _dot(a, b):
    return jnp.dot(a, b, preferred_element_type=F32)


def _dot_nt(a, b):
    return lax.dot_general(a, b, (((1,), (1,)), ((), ())), preferred_element_type=F32)


KEY_TILE = 256


def _proj_kernel(x_ref, w_ref, *out_refs, splits, n_plain):
    xb = x_ref[...].astype(BF16)
    for n, ((start, width), o_ref) in enumerate(zip(splits, out_refs)):
        y = _dot(xb, w_ref[:, start:start + width]).astype(o_ref.dtype)
        if n < n_plain:
            o_ref[...] = y
        else:
            for s in range(o_ref.shape[0]):
                o_ref[s] = y[s * KEY_TILE:(s + 1) * KEY_TILE, :].astype(F32).T.astype(o_ref.dtype)


def _proj(x, w, splits, dtypes, n_plain, tm=1024):
    m, k = x.shape
    n = w.shape[1]
    plain, tposed = splits[:n_plain], splits[n_plain:]
    return pl.pallas_call(
        functools.partial(_proj_kernel, splits=tuple(splits), n_plain=n_plain),
        out_shape=[jax.ShapeDtypeStruct((m, wd), dt) for (_, wd), dt in zip(plain, dtypes)]
        + [jax.ShapeDtypeStruct((m // KEY_TILE, wd, KEY_TILE), dt) for (_, wd), dt in zip(tposed, dtypes[n_plain:])],
        grid=(m // tm,),
        in_specs=[pl.BlockSpec((tm, k), lambda i: (i, 0)),
                  pl.BlockSpec((k, n), lambda i: (0, 0))],
        out_specs=[pl.BlockSpec((tm, wd), lambda i: (i, 0)) for (_, wd) in plain]
        + [pl.BlockSpec((tm // KEY_TILE, wd, KEY_TILE), lambda i: (i, 0, 0)) for (_, wd) in tposed],
        compiler_params=_cparams("parallel"),
        name="proj_in",
    )(x, w)


SB_UNDERFLOW = 150.0


def _sb_kernel(q_ref, k_ref, vt_ref, o_ref, *, t, nh):
    i = pl.program_id(2)
    lane = lax.broadcasted_iota(jnp.int32, (1, LANES), 1)
    rr = lax.broadcasted_iota(jnp.int32, (t, t), 0)
    cc = lax.broadcasted_iota(jnp.int32, (t, t), 1)
    before = rr < cc
    later = jnp.where(before, 1.0, 0.0).astype(BF16)
    hs = range(nh)
    pair = lambda h: slice((h // 2) * LANES, (h // 2 + 1) * LANES)
    qh = [q_ref[0, :, pair(h)] * jnp.where(lane // HEAD_DIM == h % 2, 1.0, 0.0).astype(BF16) for h in hs]

    def tile(j, state, diag):
        r0 = pl.multiple_of(j * t, t)
        z = [_dot_nt(k_ref[0, pl.ds(r0, t), pair(h)], qh[h]) for h in hs]
        nz = [-z[h] for h in hs]
        lg = [jnp.log2(1.0 + jnp.exp2(jnp.minimum(z[h], nz[h]))) for h in hs]
        lb = [jnp.minimum(nz[h], 0.0) - lg[h] for h in hs]
        lr = [jnp.where(before, lb[h], 0.0) for h in hs] if diag else lb
        hi = [lr[h].astype(BF16) for h in hs]
        hl = [jnp.concatenate([hi[h], (lr[h] - hi[h].astype(F32)).astype(BF16)], axis=1) for h in hs]
        bt = [_dot(later, hl[h]) for h in hs]
        w = [jnp.exp2(lb[h] + z[h] + (bt[h][:, :t] + bt[h][:, t:] + state[h][0])) for h in hs]
        if diag:
            w = [jnp.where(before, w[h], 0.0) for h in hs]
        pv = [_dot(vt_ref[0, j, h * HEAD_DIM:(h + 1) * HEAD_DIM, :], w[h].astype(BF16)) for h in hs]
        return tuple((state[h][0] + jnp.sum(lr[h], axis=0, keepdims=True), state[h][1] + pv[h]) for h in hs)

    zero = (jnp.zeros((1, t), F32), jnp.zeros((HEAD_DIM, t), F32))
    state = tile(i, (zero,) * nh, True)

    def cond(c):
        return (c[0] >= 0) & (c[1] > -SB_UNDERFLOW)

    def body(c):
        st = tile(c[0], c[2], False)
        top = st[0][0]
        for h in range(1, nh):
            top = jnp.maximum(top, st[h][0])
        return c[0] - 1, jnp.max(top), st

    _, _, state = lax.while_loop(cond, body, (i - 1, jnp.float32(0.0), state))
    for p in range(nh // 2):
        o_ref[0, :, p * LANES:(p + 1) * LANES] = jnp.concatenate(
            [state[2 * p][1], state[2 * p + 1][1]], axis=0).T.astype(o_ref.dtype)


def _sb_attention(qk, vt, nh=SB_HEADS):
    b, s, _ = qk.shape
    t = vt.shape[-1]
    wb = nh * HEAD_DIM
    nblk = SB_WIDTH // wb
    return pl.pallas_call(
        functools.partial(_sb_kernel, t=t, nh=nh),
        out_shape=jax.ShapeDtypeStruct((b, s, SB_WIDTH), BF16),
        grid=(b, nblk, s // t),
        in_specs=[pl.BlockSpec((1, t, wb), lambda bi, hb, i: (bi, i, hb)),
                  pl.BlockSpec((1, s, wb), lambda bi, hb, i: (bi, 0, nblk + hb)),
                  pl.BlockSpec((1, s // t, wb, t), lambda bi, hb, i: (bi, 0, hb, 0))],
        out_specs=pl.BlockSpec((1, t, wb), lambda bi, hb, i: (bi, i, hb)),
        compiler_params=_cparams("parallel", "parallel", "arbitrary"),
        name="sb_attention",
    )(qk, qk, vt)


CONV_PAD = 32


def _proj_conv_kernel(x_ref, w_ref, cw_ref, cb_ref, clg_ref, clb_ref, qk_ref, vt_ref, cv_ref, hist, *,
                      tiles_per_seq, sub):
    i = pl.program_id(0)
    tm = x_ref.shape[0]
    xb = x_ref[...].astype(BF16)
    ag = _dot(xb, w_ref[:, 0:2 * CONV_CH])
    qk = _dot(xb, w_ref[:, 2 * CONV_CH:2 * CONV_CH + 2 * SB_WIDTH])
    v = _dot(xb, w_ref[:, 2 * CONV_CH + 2 * SB_WIDTH:])
    glu = ag[:, :CONV_CH] * _sigmoid(ag[:, CONV_CH:])
    prev = jnp.where(i % tiles_per_seq == 0, 0.0, hist[...])
    hext = jnp.concatenate([prev, glu], axis=0)
    hist[...] = glu[tm - CONV_PAD:, :]
    off = CONV_PAD - (CONV_WIDTH - 1)
    for r0 in range(0, tm, sub):
        acc = jnp.zeros((sub, CONV_CH), F32) + cb_ref[...]
        for ph in range(SUBLANES):
            taps = [w for w in range(CONV_WIDTH) if (off + w) % SUBLANES == ph]
            rows = sub + (SUBLANES if ph else 0)
            part = None
            for w in taps:
                a0 = r0 + (off + w) // SUBLANES * SUBLANES
                term = cw_ref[w:w + 1, :] * hext[a0:a0 + rows, :]
                part = term if part is None else part + term
            acc = acc + part[ph:ph + sub, :]
        y = _layer_norm(acc, clg_ref[...], clb_ref[...])
        cv_ref[r0:r0 + sub, :] = (y * _sigmoid(y)).astype(cv_ref.dtype)
    qk_ref[...] = qk.astype(qk_ref.dtype)
    vb = v.astype(vt_ref.dtype)
    for s in range(vt_ref.shape[0]):
        vt_ref[s] = vb[s * KEY_TILE:(s + 1) * KEY_TILE, :].astype(F32).T.astype(vt_ref.dtype)


def _proj_conv(x, w, conv_w, conv_b, conv_ln_g, conv_ln_b, seq, tm=1024, sub=128):
    m, k = x.shape
    n = w.shape[1]
    assert seq % tm == 0 and tm % sub == 0
    cvec = lambda v: v.reshape(1, CONV_CH).astype(F32)
    full = lambda shape: pl.BlockSpec(shape, lambda i: (0, 0))
    return pl.pallas_call(
        functools.partial(_proj_conv_kernel, tiles_per_seq=seq // tm, sub=sub),
        out_shape=[jax.ShapeDtypeStruct((m, 2 * SB_WIDTH), BF16),
                   jax.ShapeDtypeStruct((m // KEY_TILE, SB_WIDTH, KEY_TILE), BF16),
                   jax.ShapeDtypeStruct((m, CONV_CH), BF16)],
        grid=(m // tm,),
        in_specs=[pl.BlockSpec((tm, k), lambda i: (i, 0)), full((k, n)), full((CONV_WIDTH, CONV_CH)),
                  full((1, CONV_CH)), full((1, CONV_CH)), full((1, CONV_CH))],
        out_specs=[pl.BlockSpec((tm, 2 * SB_WIDTH), lambda i: (i, 0)),
                   pl.BlockSpec((tm // KEY_TILE, SB_WIDTH, KEY_TILE), lambda i: (i, 0, 0)),
                   pl.BlockSpec((tm, CONV_CH), lambda i: (i, 0))],
        scratch_shapes=[pltpu.VMEM((CONV_PAD, CONV_CH), F32)],
        compiler_params=_cparams("arbitrary"),
        name="proj_conv",
    )(x, w, conv_w.astype(F32), cvec(conv_b), cvec(conv_ln_g), cvec(conv_ln_b))


FF_CHUNK = 256


def _post_mixer_kernel(l_ref, r_ref, wo_ref, x_ref, g1_ref, b1_ref, wg_ref, wu_ref, wd_ref, g2_ref, b2_ref, o_ref):
    tm, kl = l_ref.shape
    hm = tm // 2
    halves = (slice(0, hm), slice(hm, tm))
    n_chunks = D_FF // FF_CHUNK
    chunk = lambda n: slice(n * FF_CHUNK, (n + 1) * FF_CHUNK)
    m = [_dot(l_ref[rows, :], wo_ref[0:kl, :]) + _dot(r_ref[rows, :], wo_ref[kl:, :]) for rows in halves]
    x, xb, gate, up = [None, None], [None, None], [None, None], [None, None]
    for p, rows in enumerate(halves):
        x[p] = _layer_norm(ALPHA * x_ref[rows, :] + m[p], g1_ref[...], b1_ref[...])
        xb[p] = x[p].astype(BF16)
        gate[p] = _dot(xb[p], wg_ref[:, chunk(0)])
        up[p] = _dot(xb[p], wu_ref[:, chunk(0)])
    xb = jnp.concatenate(xb, axis=0)
    gate = jnp.concatenate(gate, axis=0)
    up = jnp.concatenate(up, axis=0)
    acc = jnp.zeros((tm, x_ref.shape[1]), F32)
    for n in range(n_chunks):
        if n > 0:
            gate = _dot(xb, wg_ref[:, chunk(n)])
            up = _dot(xb, wu_ref[:, chunk(n)])
        h = (gate * _sigmoid(gate) * up).astype(BF16)
        if n < n_chunks - 1:
            acc = acc + _dot(h, wd_ref[chunk(n), :])
    for p, rows in enumerate(halves):
        y = acc[rows, :] + _dot(h[rows, :], wd_ref[chunk(n_chunks - 1), :])
        o_ref[rows, :] = _layer_norm(ALPHA * x[p] + y, g2_ref[...], b2_ref[...])


def _post_mixer(left, right, wo, x, g1, b1, wg, wu, wd, g2, b2, tm=512):
    m, d = x.shape
    kl, kr = left.shape[1], right.shape[1]
    ff = wg.shape[1]
    row = lambda w: pl.BlockSpec((tm, w), lambda i: (i, 0))
    res = lambda shape: pl.BlockSpec(shape, lambda i: (0, 0), pipeline_mode=pl.Buffered(1))
    vec = lambda v: v.reshape(1, d).astype(F32)
    return pl.pallas_call(
        _post_mixer_kernel,
        out_shape=jax.ShapeDtypeStruct((m, d), F32),
        grid=(m // tm,),
        in_specs=[row(kl), row(kr), res((kl + kr, d)), row(d), res((1, d)), res((1, d)),
                  res((d, ff)), res((d, ff)), res((ff, d)), res((1, d)), res((1, d))],
        out_specs=row(d),
        compiler_params=_cparams("parallel"),
        name="post_mixer",
    )(left, right, wo, x, vec(g1), vec(b1), wg, wu, wd, vec(g2), vec(b2))


def _compress_kernel(xk_ref, xv_ref, pos_ref, w1_ref, w2_ref, kc_ref, vct_ref):
    n = xk_ref.shape[1] // CMP_STRIDE
    out = []
    for s, x_ref in enumerate((xk_ref, xv_ref)):
        xs = [x_ref[0, pl.ds(r, n, stride=CMP_STRIDE), :] for r in range(CMP_STRIDE)]
        half = lambda o: jnp.concatenate([x + pos_ref[s, o + r:o + r + 1, :] for r, x in enumerate(xs)], axis=1)
        top = _dot(half(0).astype(BF16), w1_ref[s, 0])
        bot = _dot(half(CMP_STRIDE).astype(BF16), w1_ref[s, 1])
        h = top + pltpu.roll(bot, n - 1, 0)
        out.append(_dot(_gelu_tanh(h).astype(BF16), w2_ref[s]))
    kc_ref[0] = out[0].astype(kc_ref.dtype)
    vct_ref[0] = out[1].astype(vct_ref.dtype).astype(F32).T.astype(vct_ref.dtype)


def _block_diag2(a):
    z = jnp.zeros_like(a)
    return jnp.concatenate([jnp.concatenate([a, z], axis=-1), jnp.concatenate([z, a], axis=-1)], axis=-2)


def _compress(cv, pos_k, pos_v, w1k, w1v, w2k, w2v):
    b, s, _ = cv.shape
    n = s // CMP_STRIDE
    pos = jnp.stack([jnp.concatenate([p, p], axis=-1) for p in (pos_k, pos_v)]).astype(F32)
    w1 = jnp.stack([_block_diag2(w.reshape(CMP_BLOCK, HEAD_DIM, HEAD_DIM)) for w in (w1k, w1v)])
    w1 = w1.reshape(2, 2, CMP_STRIDE * KV_WIDTH, KV_WIDTH).astype(BF16)
    w2 = jnp.stack([_block_diag2(w) for w in (w2k, w2v)]).astype(BF16)
    full = lambda a: pl.BlockSpec(a.shape, lambda bi: (0,) * a.ndim)
    return pl.pallas_call(
        _compress_kernel,
        out_shape=[jax.ShapeDtypeStruct((b, n, KV_WIDTH), BF16), jax.ShapeDtypeStruct((b, KV_WIDTH, n), BF16)],
        grid=(b,),
        in_specs=[pl.BlockSpec((1, s, KV_WIDTH), lambda bi: (bi, 0, 0)),
                  pl.BlockSpec((1, s, KV_WIDTH), lambda bi: (bi, 0, 1)), full(pos), full(w1), full(w2)],
        out_specs=[pl.BlockSpec((1, n, KV_WIDTH), lambda bi: (bi, 0, 0)),
                   pl.BlockSpec((1, KV_WIDTH, n), lambda bi: (bi, 0, 0))],
        compiler_params=_cparams("parallel"),
        name="nsa_compress",
    )(cv, cv, pos, w1, w2)


NSA_GROUP = 4


def _nsa_kernel(q_ref, ks_ref, vst_ref, kw_ref, vwt_ref, kc_ref, vct_ref, gt_ref, o_ref, *, t, n_cmp):
    i = pl.program_id(1)
    nh = NSA_GROUP
    lane = lax.broadcasted_iota(jnp.int32, (1, LANES), 1)
    tpos = i * t + lax.broadcasted_iota(jnp.int32, (1, t), 1)
    krow = lax.broadcasted_iota(jnp.int32, (t, 1), 0)
    qg = []
    for g in range(2):
        hm = jnp.where(lane // HEAD_DIM == g, 1.0, 0.0).astype(BF16)
        qg.append(jnp.concatenate([q_ref[0, :, m * LANES:(m + 1) * LANES] * hm for m in range(nh)], axis=0))

    nblk = lax.broadcasted_iota(jnp.int32, (LANES, 1), 0)
    cvalid = (nblk * CMP_STRIDE + (CMP_BLOCK - 1) <= tpos) & (nblk < n_cmp)
    jr = lax.broadcasted_iota(jnp.int32, (SEL_LANES, LANES), 0)
    nc = lax.broadcasted_iota(jnp.int32, (SEL_LANES, LANES), 1)
    ovl_t = jnp.where((nc * CMP_STRIDE < jr * SEL_BLOCK + SEL_BLOCK) & (nc * CMP_STRIDE + CMP_BLOCK > jr * SEL_BLOCK)
                      & (nc < n_cmp), 1.0, 0.0).astype(BF16)
    jsel = lax.broadcasted_iota(jnp.int32, (SEL_LANES, 1), 0)
    blk_t = tpos // SEL_BLOCK
    forced = (jsel == 0) | (jsel == blk_t) | (jsel == blk_t - 1)
    pad_rows = lax.broadcasted_iota(jnp.int32, (LANES - SEL_LANES, 1), 0)
    o_cmp = []
    qx = []
    for g in range(2):
        st = _dot_nt(kc_ref[0], qg[g])
        ps = []
        psum = jnp.zeros((LANES, t), F32)
        for m in range(nh):
            s = jnp.where(cvalid, st[:, m * t:(m + 1) * t], NEG)
            e = jnp.where(cvalid, jnp.exp2(s - jnp.max(s, axis=0, keepdims=True)), 0.0)
            l = jnp.sum(e, axis=0, keepdims=True)
            p = e * (1.0 / jnp.where(l > 0.0, l, 1.0))
            ps.append(p.astype(BF16))
            psum = psum + p
        o_cmp.append(_dot(vct_ref[0, g * HEAD_DIM:(g + 1) * HEAD_DIM, :], jnp.concatenate(ps, axis=1)))
        hi = psum.astype(BF16)
        lo = (psum - hi.astype(F32)).astype(BF16)
        it = _dot(ovl_t, jnp.concatenate([hi, lo], axis=1))
        imp = it[:, :t] + it[:, t:]
        score = jnp.where(jsel <= blk_t, imp + jnp.where(forced, FORCE_BONUS, 0.0), NEG)
        cnt = jnp.zeros((SEL_LANES, t), F32)
        for c in range(SEL_LANES):
            row = score[c:c + 1, :]
            cnt = cnt + jnp.where((row > score) | ((row == score) & (jsel > c)), 1.0, 0.0)
        tail = jnp.where(pad_rows == NSA_BIAS_LANE - SEL_LANES, NEG, 0.0) + jnp.zeros((1, t), F32)
        selt = jnp.concatenate([jnp.where(cnt < SEL_TOPK, 0.0, NEG), tail], axis=0).T.astype(BF16)
        qx.append(jnp.concatenate([qg[g], jnp.concatenate([selt] * nh, axis=0)], axis=1))

    def attend(streams, scores=None):
        if scores is None:
            scores = [_dot_nt(k, q) for k, _, q, _, _ in streams]
        sb = [s if bias is None else s + jnp.concatenate([bias] * nh, axis=1)
              for s, (_, _, _, bias, _) in zip(scores, streams)]
        mnew = [jnp.maximum(st[0], jnp.max(s, axis=0, keepdims=True)) for s, (_, _, _, _, st) in zip(sb, streams)]
        p = [jnp.exp2(s - mn) for s, mn in zip(sb, mnew)]
        a = [jnp.exp2(st[0] - mn) for mn, (_, _, _, _, st) in zip(mnew, streams)]
        pv = [_dot(vt, pp.astype(BF16)) for pp, (_, vt, _, _, _) in zip(p, streams)]
        return [(mn, aa * st[1] + jnp.sum(pp, axis=0, keepdims=True), aa * st[2] + o)
                for mn, aa, pp, o, (_, _, _, _, st) in zip(mnew, a, p, pv, streams)]

    init = (jnp.full((1, nh * t), NEG, F32), jnp.zeros((1, nh * t), F32), jnp.zeros((HEAD_DIM, nh * t), F32))
    per_tile = t // SEL_BLOCK
    rows_g = lambda g: slice(g * HEAD_DIM, (g + 1) * HEAD_DIM)

    def slc_streams(j, jc, state, bias):
        onehot = (lane == j * per_tile + krow // SEL_BLOCK) | ((lane == NSA_BIAS_LANE) & (j < 0))
        r0 = pl.multiple_of(jc * t, t)
        kx = jnp.concatenate([ks_ref[0, pl.ds(r0, t), :], jnp.where(onehot, 1.0, 0.0).astype(BF16)], axis=1)
        return [(kx, vst_ref[0, jc, rows_g(g), :], qx[g], bias, state[g]) for g in range(2)]

    slc = [init, init]
    win = [init, init]
    n_win = WINDOW // t + 1
    for d in range(n_win):
        j = i - d
        jc = jnp.maximum(j, 0)
        kpos = j * t + krow
        wbias = jnp.where((kpos >= 0) & (kpos <= tpos) & (tpos - kpos < WINDOW), 0.0, NEG)
        causal = jnp.where(kpos <= tpos, 0.0, NEG) if d == 0 else None
        r0 = pl.multiple_of(jc * t, t)
        sstreams = slc_streams(j, jc, slc, causal)
        wstreams = [(None, vwt_ref[0, jc, rows_g(g), :], None, wbias, win[g]) for g in range(2)]
        kwx = jnp.concatenate([kw_ref[0, pl.ds(r0, t), :], jnp.zeros((t, LANES), BF16)], axis=1)
        both = [_dot_nt(jnp.concatenate([sstreams[g][0], kwx], axis=0), qx[g]) for g in range(2)]
        res = attend(sstreams + wstreams, [both[0][:t], both[1][:t], both[0][t:], both[1][t:]])
        slc, win = res[:2], res[2:]

    def slc_body(it, state):
        j = i - it
        return tuple(attend(slc_streams(j, j, state, None)))

    slc = lax.fori_loop(n_win, i + 1, slc_body, tuple(slc))

    gates = _sigmoid(gt_ref[0]).T
    for m in range(nh):
        halves = []
        for g in range(2):
            c = g * 3 * nh + m * 3
            sl = slice(m * t, (m + 1) * t)
            o_s = slc[g][2][:, sl] * (1.0 / slc[g][1][:, sl])
            o_w = win[g][2][:, sl] * (1.0 / win[g][1][:, sl])
            halves.append(gates[c:c + 1, :] * o_cmp[g][:, sl] + gates[c + 1:c + 2, :] * o_s + gates[c + 2:c + 3, :] * o_w)
        o_ref[0, :, m * LANES:(m + 1) * LANES] = jnp.concatenate(halves, axis=0).T.astype(o_ref.dtype)


def _nsa_attention(qk, vt, kc2, vct, gt, n_cmp):
    b, s, _ = qk.shape
    t = vt.shape[-1]
    assert WINDOW % t == 0 and t % SEL_BLOCK == 0
    nt = s // t
    kspec = lambda col: pl.BlockSpec((1, s, LANES), lambda bi, i: (bi, 0, col))
    vspec = lambda blk: pl.BlockSpec((1, nt, LANES, t), lambda bi, i: (bi, 0, blk, 0))
    cspec = pl.BlockSpec((1, LANES, LANES), lambda bi, i: (bi, 0, 0))
    return pl.pallas_call(
        functools.partial(_nsa_kernel, t=t, n_cmp=n_cmp),
        out_shape=jax.ShapeDtypeStruct((b, s, NSA_WIDTH), BF16),
        grid=(b, nt),
        in_specs=[pl.BlockSpec((1, t, NSA_WIDTH), lambda bi, i: (bi, i, 0)),
                  kspec(4), vspec(0), kspec(5), vspec(1), cspec, cspec,
                  pl.BlockSpec((1, t, LANES), lambda bi, i: (bi, i, 0))],
        out_specs=pl.BlockSpec((1, t, NSA_WIDTH), lambda bi, i: (bi, i, 0)),
        compiler_params=_cparams("parallel", "arbitrary"),
        name="nsa_attention",
    )(qk, qk, vt, qk, vt, kc2, vct, gt)


def _gmlp_kernel(u_ref, v_ref, lg_ref, lb_ref, ws_ref, bs_ref, o_ref, *, tc):
    cg = GMLP_WIDTH // GMLP_GROUPS
    u = _gelu_tanh(u_ref[0])
    v = _layer_norm(_gelu_tanh(v_ref[0]), lg_ref[...], lb_ref[...]).astype(BF16)
    rr = lax.broadcasted_iota(jnp.int32, (GMLP_CHUNK, GMLP_CHUNK), 0)
    cc = lax.broadcasted_iota(jnp.int32, (GMLP_CHUNK, GMLP_CHUNK), 1)
    for g in range(GMLP_GROUPS):
        wm = jnp.where(rr >= cc, ws_ref[g], 0.0).astype(BF16)
        bias = bs_ref[:, g:g + 1]
        for c0 in range(0, tc, GMLP_CHUNK):
            mixed = _dot(wm, v[c0:c0 + GMLP_CHUNK, g * cg:(g + 1) * cg]) + bias
            o_ref[0, c0:c0 + GMLP_CHUNK, g * cg:(g + 1) * cg] = (
                u[c0:c0 + GMLP_CHUNK, g * cg:(g + 1) * cg] * mixed).astype(o_ref.dtype)


def _gmlp(uv, ln_g, ln_b, ws, bs, tc=1024):
    b, s, _ = uv.shape
    vec = lambda a: a.reshape(1, GMLP_WIDTH).astype(F32)
    return pl.pallas_call(
        functools.partial(_gmlp_kernel, tc=tc),
        out_shape=jax.ShapeDtypeStruct((b, s, GMLP_WIDTH), BF16),
        grid=(b, s // tc),
        in_specs=[pl.BlockSpec((1, tc, GMLP_WIDTH), lambda bi, c: (bi, c, 0)),
                  pl.BlockSpec((1, tc, GMLP_WIDTH), lambda bi, c: (bi, c, 1)),
                  pl.BlockSpec((1, GMLP_WIDTH), lambda bi, c: (0, 0)),
                  pl.BlockSpec((1, GMLP_WIDTH), lambda bi, c: (0, 0)),
                  pl.BlockSpec((GMLP_GROUPS, GMLP_CHUNK, GMLP_CHUNK), lambda bi, c: (0, 0, 0)),
                  pl.BlockSpec((GMLP_CHUNK, GMLP_GROUPS), lambda bi, c: (0, 0))],
        out_specs=pl.BlockSpec((1, tc, GMLP_WIDTH), lambda bi, c: (bi, c, 0)),
        compiler_params=_cparams("parallel", "parallel"),
        name="gmlp",
    )(uv, uv, vec(ln_g), vec(ln_b), ws.astype(F32), jnp.transpose(bs).astype(F32))


def _head_pair_perm():
    m, g, d = np.meshgrid(np.arange(4), np.arange(2), np.arange(HEAD_DIM), indexing="ij")
    return ((g * 4 + m) * HEAD_DIM + d).reshape(-1)


def _even_layer(x, w_in, conv_w, conv_b, conv_ln_g, conv_ln_b, w_out, post, b, s):
    w_sc = jnp.concatenate([w_in[:, :SB_WIDTH] * (QK_SCALE * LOG2E), w_in[:, SB_WIDTH:]], axis=1)
    w_sc = jnp.concatenate([w_sc[:, 3 * SB_WIDTH:], w_sc[:, :3 * SB_WIDTH]], axis=1)
    qk, vt, o_cv = _proj_conv(x, w_sc.astype(BF16), conv_w, conv_b, conv_ln_g, conv_ln_b, s)
    o_sb = _sb_attention(qk.reshape(b, s, -1), vt.reshape(b, s // KEY_TILE, SB_WIDTH, KEY_TILE))
    return _post_mixer(o_sb.reshape(b * s, -1), o_cv.reshape(b * s, -1), w_out.astype(BF16), x, *post)


def _odd_layer(x, w_in, cmpk_pos, cmpk_w1, cmpk_w2, cmpv_pos, cmpv_w1, cmpv_w2,
               gmlp_ln_g, gmlp_ln_b, gmlp_ws, gmlp_bs, w_out, post, b, s):
    perm = _head_pair_perm()
    o_gt, o_u = NSA_WIDTH + 6 * KV_WIDTH, NSA_WIDTH + 6 * KV_WIDTH + 24
    w_gt = jnp.pad(w_in[:, o_gt:o_u], ((0, 0), (0, LANES - 24)))
    col = lambda n: w_in[:, NSA_WIDTH + n * KV_WIDTH:NSA_WIDTH + (n + 1) * KV_WIDTH]
    w_re = jnp.concatenate([w_in[:, perm] * (QK_SCALE * LOG2E), col(2), col(4), col(0), col(1), w_in[:, o_u:], w_gt,
                            col(3), col(5)], axis=1)
    qk, cv, uv, gt, vt = _proj(
        x, w_re.astype(BF16), [(0, 768), (768, 256), (1024, 1024), (2048, LANES), (2176, 2 * KV_WIDTH)],
        [BF16, F32, F32, F32, BF16], n_plain=4)
    n_cmp = (s - CMP_BLOCK) // CMP_STRIDE + 1
    kc2, vct = _compress(cv.reshape(b, s, -1), cmpk_pos, cmpv_pos, cmpk_w1, cmpv_w1, cmpk_w2, cmpv_w2)
    o_nsa = _nsa_attention(qk.reshape(b, s, -1), vt.reshape(b, s // KEY_TILE, 2 * KV_WIDTH, KEY_TILE), kc2, vct,
                           gt.reshape(b, s, -1), n_cmp)
    o_mlp = _gmlp(uv.reshape(b, s, -1), gmlp_ln_g, gmlp_ln_b, gmlp_ws, gmlp_bs)
    w_out_re = jnp.concatenate([w_out[:NSA_WIDTH][perm], w_out[NSA_WIDTH:]], axis=0)
    return _post_mixer(o_nsa.reshape(b * s, -1), o_mlp.reshape(b * s, -1), w_out_re.astype(BF16), x, *post)


def kernel(x, ev_w_in, ev_conv_w, ev_conv_b, ev_conv_ln_g, ev_conv_ln_b, ev_w_out, od_w_in, od_cmpk_pos, od_cmpk_w1, od_cmpk_w2, od_cmpv_pos, od_cmpv_w1, od_cmpv_w2, od_gmlp_ln_g, od_gmlp_ln_b, od_gmlp_ws, od_gmlp_bs, od_w_out, ffn_w_gate, ffn_w_up, ffn_w_down, ln1_g, ln1_b, ln2_g, ln2_b):
    b, s, d = x.shape
    assert s // CMP_STRIDE == LANES and s // SEL_BLOCK == SEL_LANES
    h = x.reshape(b * s, d)
    for layer in range(DEPTH):
        i = layer // 2
        post = (ln1_g[layer], ln1_b[layer], ffn_w_gate[layer].astype(BF16), ffn_w_up[layer].astype(BF16),
                ffn_w_down[layer].astype(BF16), ln2_g[layer], ln2_b[layer])
        if layer % 2 == 0:
            h = _even_layer(h, ev_w_in[i], ev_conv_w[i], ev_conv_b[i], ev_conv_ln_g[i], ev_conv_ln_b[i],
                            ev_w_out[i], post, b, s)
        else:
            h = _odd_layer(h, od_w_in[i], od_cmpk_pos[i], od_cmpk_w1[i], od_cmpk_w2[i],
                           od_cmpv_pos[i], od_cmpv_w1[i], od_cmpv_w2[i], od_gmlp_ln_g[i], od_gmlp_ln_b[i],
                           od_gmlp_ws[i], od_gmlp_bs[i], od_w_out[i], post, b, s)
    return h.reshape(b, s, d)
```

```python
import functools

import numpy as np
import jax
import jax.numpy as jnp
from jax import lax
from jax.experimental import pallas as pl
from jax.experimental.pallas import tpu as pltpu

F32 = jnp.float32
BF16 = jnp.bfloat16

D_MODEL = 1024
DEPTH = 2
HEAD_DIM = 64
LANES = 128
SUBLANES = 8
SB_WIDTH = 512
SB_HEADS = 8
CONV_CH = 512
CONV_WIDTH = 31
NSA_WIDTH = 512
KV_WIDTH = 128
CMP_BLOCK = 32
CMP_STRIDE = 16
SEL_BLOCK = 64
SEL_TOPK = 8
SEL_LANES = 32
WINDOW = 512
GMLP_WIDTH = 512
GMLP_GROUPS = 4
GMLP_CHUNK = 128
D_FF = 2816
ALPHA = (2 * DEPTH) ** 0.25
LN_EPS = 1e-5
NEG = -1e30
FORCE_BONUS = 1e3
QK_SCALE = HEAD_DIM ** -0.5
LOG2E = float(np.log2(np.e))
NSA_BIAS_LANE = SEL_LANES

VMEM_LIMIT = 56 * 1024 * 1024


def _cparams(*sem):
    return pltpu.CompilerParams(dimension_semantics=sem, vmem_limit_bytes=VMEM_LIMIT)


def _layer_norm(y, g, b):
    mu = jnp.mean(y, axis=-1, keepdims=True)
    d = y - mu
    var = jnp.mean(d * d, axis=-1, keepdims=True)
    return d * lax.rsqrt(var + LN_EPS) * g + b


def _sigmoid(x):
    return 1.0 / (1.0 + jnp.exp2(x * -LOG2E))


def _gelu_tanh(x):
    c = np.float32(np.sqrt(2.0 / np.pi))
    return x * (0.5 * (1.0 + jnp.tanh(c * (x + 0.044715 * (x * x * x)))))


def _dot(a, b):
    return jnp.dot(a, b, preferred_element_type=F32)


def _dot_nt(a, b):
    return lax.dot_general(a, b, (((1,), (1,)), ((), ())), preferred_element_type=F32)


KEY_TILE = 256


SB_UNDERFLOW = 150.0


def _sb_kernel(q_ref, k_ref, vt_ref, o_ref, *, t, nh):
    i = pl.program_id(2)
    lane = lax.broadcasted_iota(jnp.int32, (1, LANES), 1)
    rr = lax.broadcasted_iota(jnp.int32, (t, t), 0)
    cc = lax.broadcasted_iota(jnp.int32, (t, t), 1)
    before = rr < cc
    later = jnp.where(before, 1.0, 0.0).astype(BF16)
    hs = range(nh)
    pair = lambda h: slice((h // 2) * LANES, (h // 2 + 1) * LANES)
    qh = [q_ref[0, :, pair(h)] * jnp.where(lane // HEAD_DIM == h % 2, 1.0, 0.0).astype(BF16) for h in hs]

    def tile(j, state, diag):
        r0 = pl.multiple_of(j * t, t)
        z = [_dot_nt(k_ref[0, pl.ds(r0, t), pair(h)], qh[h]) for h in hs]
        nz = [-z[h] for h in hs]
        lg = [jnp.log2(1.0 + jnp.exp2(jnp.minimum(z[h], nz[h]))) for h in hs]
        lb = [jnp.minimum(nz[h], 0.0) - lg[h] for h in hs]
        lr = [jnp.where(before, lb[h], 0.0) for h in hs] if diag else lb
        hi = [lr[h].astype(BF16) for h in hs]
        hl = [jnp.concatenate([hi[h], (lr[h] - hi[h].astype(F32)).astype(BF16)], axis=1) for h in hs]
        bt = [_dot(later, hl[h]) for h in hs]
        w = [jnp.exp2(lb[h] + z[h] + (bt[h][:, :t] + bt[h][:, t:] + state[h][0])) for h in hs]
        if diag:
            w = [jnp.where(before, w[h], 0.0) for h in hs]
        pv = [_dot(vt_ref[0, j, h * HEAD_DIM:(h + 1) * HEAD_DIM, :], w[h].astype(BF16)) for h in hs]
        return tuple((state[h][0] + jnp.sum(lr[h], axis=0, keepdims=True), state[h][1] + pv[h]) for h in hs)

    zero = (jnp.zeros((1, t), F32), jnp.zeros((HEAD_DIM, t), F32))
    state = tile(i, (zero,) * nh, True)

    def cond(c):
        return (c[0] >= 0) & (c[1] > -SB_UNDERFLOW)

    def body(c):
        st = tile(c[0], c[2], False)
        top = st[0][0]
        for h in range(1, nh):
            top = jnp.maximum(top, st[h][0])
        return c[0] - 1, jnp.max(top), st

    _, _, state = lax.while_loop(cond, body, (i - 1, jnp.float32(0.0), state))
    for p in range(nh // 2):
        o_ref[0, :, p * LANES:(p + 1) * LANES] = jnp.concatenate(
            [state[2 * p][1], state[2 * p + 1][1]], axis=0).T.astype(o_ref.dtype)


def _sb_attention(qk, vt, nh=SB_HEADS):
    b, s, _ = qk.shape
    t = vt.shape[-1]
    wb = nh * HEAD_DIM
    nblk = SB_WIDTH // wb
    return pl.pallas_call(
        functools.partial(_sb_kernel, t=t, nh=nh),
        out_shape=jax.ShapeDtypeStruct((b, s, SB_WIDTH), BF16),
        grid=(b, nblk, s // t),
        in_specs=[pl.BlockSpec((1, t, wb), lambda bi, hb, i: (bi, i, hb)),
                  pl.BlockSpec((1, s, wb), lambda bi, hb, i: (bi, 0, nblk + hb)),
                  pl.BlockSpec((1, s // t, wb, t), lambda bi, hb, i: (bi, 0, hb, 0))],
        out_specs=pl.BlockSpec((1, t, wb), lambda bi, hb, i: (bi, i, hb)),
        compiler_params=_cparams("parallel", "parallel", "arbitrary"),
        name="sb_attention",
    )(qk, qk, vt)


CONV_PAD = 32


def _proj_conv_kernel(x_ref, w_ref, cw_ref, cb_ref, clg_ref, clb_ref, *rest, tiles_per_seq, sub, n_cast):
    casts_in, (qk_ref, vt_ref, cv_ref), casts_out, hist = (
        rest[:n_cast], rest[n_cast:n_cast + 3], rest[n_cast + 3:2 * n_cast + 3], rest[-1])
    for src_ref, dst_ref in zip(casts_in, casts_out):
        dst_ref[...] = src_ref[...].astype(dst_ref.dtype)
    i = pl.program_id(0)
    tm = x_ref.shape[0]
    xb = x_ref[...].astype(BF16)
    ag = _dot(xb, w_ref[:, 0:2 * CONV_CH])
    qk = _dot(xb, w_ref[:, 2 * CONV_CH:2 * CONV_CH + 2 * SB_WIDTH])
    v = _dot(xb, w_ref[:, 2 * CONV_CH + 2 * SB_WIDTH:])
    glu = ag[:, :CONV_CH] * _sigmoid(ag[:, CONV_CH:])
    prev = jnp.where(i % tiles_per_seq == 0, 0.0, hist[...])
    hext = jnp.concatenate([prev, glu], axis=0)
    hist[...] = glu[tm - CONV_PAD:, :]
    off = CONV_PAD - (CONV_WIDTH - 1)
    for r0 in range(0, tm, sub):
        acc = jnp.zeros((sub, CONV_CH), F32) + cb_ref[...]
        for ph in range(SUBLANES):
            taps = [w for w in range(CONV_WIDTH) if (off + w) % SUBLANES == ph]
            rows = sub + (SUBLANES if ph else 0)
            part = None
            for w in taps:
                a0 = r0 + (off + w) // SUBLANES * SUBLANES
                term = cw_ref[w:w + 1, :] * hext[a0:a0 + rows, :]
                part = term if part is None else part + term
            acc = acc + part[ph:ph + sub, :]
        y = _layer_norm(acc, clg_ref[...], clb_ref[...])
        cv_ref[r0:r0 + sub, :] = (y * _sigmoid(y)).astype(cv_ref.dtype)
    qk_ref[...] = qk.astype(qk_ref.dtype)
    vb = v.astype(vt_ref.dtype)
    for s in range(vt_ref.shape[0]):
        vt_ref[s] = vb[s * KEY_TILE:(s + 1) * KEY_TILE, :].astype(F32).T.astype(vt_ref.dtype)


def _proj_conv(x, w, conv_w, conv_b, conv_ln_g, conv_ln_b, seq, casts=(), tm=1024, sub=128):
    m, k = x.shape
    n = w.shape[1]
    steps = m // tm
    assert seq % tm == 0 and tm % sub == 0 and all(c.shape[0] % (16 * steps) == 0 for c in casts)
    cvec = lambda v: v.reshape(1, CONV_CH).astype(F32)
    full = lambda shape: pl.BlockSpec(shape, lambda i: (0, 0))
    slab = lambda c: pl.BlockSpec((c.shape[0] // steps, c.shape[1]), lambda i: (i, 0))
    return pl.pallas_call(
        functools.partial(_proj_conv_kernel, tiles_per_seq=seq // tm, sub=sub, n_cast=len(casts)),
        out_shape=[jax.ShapeDtypeStruct((m, 2 * SB_WIDTH), BF16),
                   jax.ShapeDtypeStruct((m // KEY_TILE, SB_WIDTH, KEY_TILE), BF16),
                   jax.ShapeDtypeStruct((m, CONV_CH), BF16)] + [jax.ShapeDtypeStruct(c.shape, BF16) for c in casts],
        grid=(steps,),
        in_specs=[pl.BlockSpec((tm, k), lambda i: (i, 0)), full((k, n)), full((CONV_WIDTH, CONV_CH)),
                  full((1, CONV_CH)), full((1, CONV_CH)), full((1, CONV_CH))] + [slab(c) for c in casts],
        out_specs=[pl.BlockSpec((tm, 2 * SB_WIDTH), lambda i: (i, 0)),
                   pl.BlockSpec((tm // KEY_TILE, SB_WIDTH, KEY_TILE), lambda i: (i, 0, 0)),
                   pl.BlockSpec((tm, CONV_CH), lambda i: (i, 0))] + [slab(c) for c in casts],
        scratch_shapes=[pltpu.VMEM((CONV_PAD, CONV_CH), F32)],
        compiler_params=_cparams("arbitrary"),
        name="proj_conv",
    )(x, w, conv_w.astype(F32), cvec(conv_b), cvec(conv_ln_g), cvec(conv_ln_b), *casts)


FF_CHUNK = 256


def _post_mixer_kernel(l_ref, r_ref, wo_ref, x_ref, g1_ref, b1_ref, wg_ref, wu_ref, wd_ref, g2_ref, b2_ref, o_ref):
    tm, kl = l_ref.shape
    hm = tm // 2
    halves = (slice(0, hm), slice(hm, tm))
    n_chunks = D_FF // FF_CHUNK
    chunk = lambda n: slice(n * FF_CHUNK, (n + 1) * FF_CHUNK)
    m = [_dot(l_ref[rows, :], wo_ref[0:kl, :]) + _dot(r_ref[rows, :], wo_ref[kl:, :]) for rows in halves]
    x, xb, gate, up = [None, None], [None, None], [None, None], [None, None]
    for p, rows in enumerate(halves):
        x[p] = _layer_norm(ALPHA * x_ref[rows, :] + m[p], g1_ref[...], b1_ref[...])
        xb[p] = x[p].astype(BF16)
        gate[p] = _dot(xb[p], wg_ref[:, chunk(0)])
        up[p] = _dot(xb[p], wu_ref[:, chunk(0)])
    xb = jnp.concatenate(xb, axis=0)
    gate = jnp.concatenate(gate, axis=0)
    up = jnp.concatenate(up, axis=0)
    acc = jnp.zeros((tm, x_ref.shape[1]), F32)
    for n in range(n_chunks):
        if n > 0:
            gate = _dot(xb, wg_ref[:, chunk(n)])
            up = _dot(xb, wu_ref[:, chunk(n)])
        h = (gate * _sigmoid(gate) * up).astype(BF16)
        if n < n_chunks - 1:
            acc = acc + _dot(h, wd_ref[chunk(n), :])
    for p, rows in enumerate(halves):
        y = acc[rows, :] + _dot(h[rows, :], wd_ref[chunk(n_chunks - 1), :])
        o_ref[rows, :] = _layer_norm(ALPHA * x[p] + y, g2_ref[...], b2_ref[...])


def _post_mixer(left, right, wo, x, g1, b1, wg, wu, wd, layer, g2, b2, tm=512):
    m, d = x.shape
    kl, kr = left.shape[1], right.shape[1]
    ff = wg.shape[1]
    row = lambda w: pl.BlockSpec((tm, w), lambda i: (i, 0))
    res = lambda shape, blk=0: pl.BlockSpec(shape, lambda i: (blk, 0), pipeline_mode=pl.Buffered(1))
    vec = lambda v: v.reshape(1, d).astype(F32)
    return pl.pallas_call(
        _post_mixer_kernel,
        out_shape=jax.ShapeDtypeStruct((m, d), F32),
        grid=(m // tm,),
        in_specs=[row(kl), row(kr), res((kl + kr, d)), row(d), res((1, d)), res((1, d)),
                  res((d, ff), layer), res((d, ff), layer), res((ff, d), layer), res((1, d)), res((1, d))],
        out_specs=row(d),
        compiler_params=_cparams("parallel"),
        name="post_mixer",
    )(left, right, wo, x, vec(g1), vec(b1), wg, wu, wd, vec(g2), vec(b2))


def _compress_kernel(xk_ref, xv_ref, pos_ref, w1_ref, w2_ref, kc_ref, vct_ref):
    n = xk_ref.shape[1] // CMP_STRIDE
    out = []
    for s, x_ref in enumerate((xk_ref, xv_ref)):
        xs = [x_ref[0, pl.ds(r, n, stride=CMP_STRIDE), :] for r in range(CMP_STRIDE)]
        half = lambda o: jnp.concatenate([x + pos_ref[s, o + r:o + r + 1, :] for r, x in enumerate(xs)], axis=1)
        top = _dot(half(0).astype(BF16), w1_ref[s, 0])
        bot = _dot(half(CMP_STRIDE).astype(BF16), w1_ref[s, 1])
        h = top + pltpu.roll(bot, n - 1, 0)
        out.append(_dot(_gelu_tanh(h).astype(BF16), w2_ref[s]))
    kc_ref[0] = out[0].astype(kc_ref.dtype)
    vct_ref[0] = out[1].astype(vct_ref.dtype).astype(F32).T.astype(vct_ref.dtype)


def _block_diag2(a):
    z = jnp.zeros_like(a)
    return jnp.concatenate([jnp.concatenate([a, z], axis=-1), jnp.concatenate([z, a], axis=-1)], axis=-2)


def _compress(cv, pos_k, pos_v, w1k, w1v, w2k, w2v):
    b, s, _ = cv.shape
    n = s // CMP_STRIDE
    pos = jnp.stack([jnp.concatenate([p, p], axis=-1) for p in (pos_k, pos_v)]).astype(F32)
    w1 = jnp.stack([_block_diag2(w.reshape(CMP_BLOCK, HEAD_DIM, HEAD_DIM)) for w in (w1k, w1v)])
    w1 = w1.reshape(2, 2, CMP_STRIDE * KV_WIDTH, KV_WIDTH).astype(BF16)
    w2 = jnp.stack([_block_diag2(w) for w in (w2k, w2v)]).astype(BF16)
    full = lambda a: pl.BlockSpec(a.shape, lambda bi: (0,) * a.ndim)
    return pl.pallas_call(
        _compress_kernel,
        out_shape=[jax.ShapeDtypeStruct((b, n, KV_WIDTH), BF16), jax.ShapeDtypeStruct((b, KV_WIDTH, n), BF16)],
        grid=(b,),
        in_specs=[pl.BlockSpec((1, s, KV_WIDTH), lambda bi: (bi, 0, 0)),
                  pl.BlockSpec((1, s, KV_WIDTH), lambda bi: (bi, 0, 1)), full(pos), full(w1), full(w2)],
        out_specs=[pl.BlockSpec((1, n, KV_WIDTH), lambda bi: (bi, 0, 0)),
                   pl.BlockSpec((1, KV_WIDTH, n), lambda bi: (bi, 0, 0))],
        compiler_params=_cparams("parallel"),
        name="nsa_compress",
    )(cv, cv, pos, w1, w2)


NSA_GROUP = 4


def _nsa_kernel(q_ref, ks_ref, vst_ref, kw_ref, vwt_ref, kc_ref, vct_ref, gt_ref, o_ref, *, t, n_cmp):
    i = pl.program_id(1)
    nh = NSA_GROUP
    lane = lax.broadcasted_iota(jnp.int32, (1, LANES), 1)
    tpos = i * t + lax.broadcasted_iota(jnp.int32, (1, t), 1)
    krow = lax.broadcasted_iota(jnp.int32, (t, 1), 0)
    qg = []
    for g in range(2):
        hm = jnp.where(lane // HEAD_DIM == g, 1.0, 0.0).astype(BF16)
        qg.append(jnp.concatenate([q_ref[0, :, m * LANES:(m + 1) * LANES] * hm for m in range(nh)], axis=0))

    nblk = lax.broadcasted_iota(jnp.int32, (LANES, 1), 0)
    cvalid = (nblk * CMP_STRIDE + (CMP_BLOCK - 1) <= tpos) & (nblk < n_cmp)
    jr = lax.broadcasted_iota(jnp.int32, (SEL_LANES, LANES), 0)
    nc = lax.broadcasted_iota(jnp.int32, (SEL_LANES, LANES), 1)
    ovl_t = jnp.where((nc * CMP_STRIDE < jr * SEL_BLOCK + SEL_BLOCK) & (nc * CMP_STRIDE + CMP_BLOCK > jr * SEL_BLOCK)
                      & (nc < n_cmp), 1.0, 0.0).astype(BF16)
    jsel = lax.broadcasted_iota(jnp.int32, (SEL_LANES, 1), 0)
    blk_t = tpos // SEL_BLOCK
    forced = (jsel == 0) | (jsel == blk_t) | (jsel == blk_t - 1)
    pad_rows = lax.broadcasted_iota(jnp.int32, (LANES - SEL_LANES, 1), 0)
    o_cmp = []
    qx = []
    for g in range(2):
        st = _dot_nt(kc_ref[0], qg[g])
        ps = []
        psum = jnp.zeros((LANES, t), F32)
        for m in range(nh):
            s = jnp.where(cvalid, st[:, m * t:(m + 1) * t], NEG)
            e = jnp.where(cvalid, jnp.exp2(s - jnp.max(s, axis=0, keepdims=True)), 0.0)
            l = jnp.sum(e, axis=0, keepdims=True)
            p = e * (1.0 / jnp.where(l > 0.0, l, 1.0))
            ps.append(p.astype(BF16))
            psum = psum + p
        o_cmp.append(_dot(vct_ref[0, g * HEAD_DIM:(g + 1) * HEAD_DIM, :], jnp.concatenate(ps, axis=1)))
        hi = psum.astype(BF16)
        lo = (psum - hi.astype(F32)).astype(BF16)
        it = _dot(ovl_t, jnp.concatenate([hi, lo], axis=1))
        imp = it[:, :t] + it[:, t:]
        score = jnp.where(jsel <= blk_t, imp + jnp.where(forced, FORCE_BONUS, 0.0), NEG)
        cnt = jnp.zeros((SEL_LANES, t), F32)
        for c in range(SEL_LANES):
            row = score[c:c + 1, :]
            cnt = cnt + jnp.where((row > score) | ((row == score) & (jsel > c)), 1.0, 0.0)
        tail = jnp.where(pad_rows == NSA_BIAS_LANE - SEL_LANES, NEG, 0.0) + jnp.zeros((1, t), F32)
        selt = jnp.concatenate([jnp.where(cnt < SEL_TOPK, 0.0, NEG), tail], axis=0).T.astype(BF16)
        qx.append(jnp.concatenate([qg[g], jnp.concatenate([selt] * nh, axis=0)], axis=1))

    def attend(streams, scores=None):
        if scores is None:
            scores = [_dot_nt(k, q) for k, _, q, _, _ in streams]
        sb = [s if bias is None else s + jnp.concatenate([bias] * nh, axis=1)
              for s, (_, _, _, bias, _) in zip(scores, streams)]
        mnew = [jnp.maximum(st[0], jnp.max(s, axis=0, keepdims=True)) for s, (_, _, _, _, st) in zip(sb, streams)]
        p = [jnp.exp2(s - mn) for s, mn in zip(sb, mnew)]
        a = [jnp.exp2(st[0] - mn) for mn, (_, _, _, _, st) in zip(mnew, streams)]
        pv = [_dot(vt, pp.astype(BF16)) for pp, (_, vt, _, _, _) in zip(p, streams)]
        return [(mn, aa * st[1] + jnp.sum(pp, axis=0, keepdims=True), aa * st[2] + o)
                for mn, aa, pp, o, (_, _, _, _, st) in zip(mnew, a, p, pv, streams)]

    init = (jnp.full((1, nh * t), NEG, F32), jnp.zeros((1, nh * t), F32), jnp.zeros((HEAD_DIM, nh * t), F32))
    per_tile = t // SEL_BLOCK
    rows_g = lambda g: slice(g * HEAD_DIM, (g + 1) * HEAD_DIM)

    def slc_streams(j, jc, state, bias):
        onehot = (lane == j * per_tile + krow // SEL_BLOCK) | ((lane == NSA_BIAS_LANE) & (j < 0))
        r0 = pl.multiple_of(jc * t, t)
        kx = jnp.concatenate([ks_ref[0, pl.ds(r0, t), :], jnp.where(onehot, 1.0, 0.0).astype(BF16)], axis=1)
        return [(kx, vst_ref[0, jc, rows_g(g), :], qx[g], bias, state[g]) for g in range(2)]

    slc = [init, init]
    win = [init, init]
    n_win = WINDOW // t + 1
    for d in range(n_win):
        j = i - d
        jc = jnp.maximum(j, 0)
        kpos = j * t + krow
        wbias = jnp.where((kpos >= 0) & (kpos <= tpos) & (tpos - kpos < WINDOW), 0.0, NEG)
        causal = jnp.where(kpos <= tpos, 0.0, NEG) if d == 0 else None
        r0 = pl.multiple_of(jc * t, t)
        sstreams = slc_streams(j, jc, slc, causal)
        wstreams = [(None, vwt_ref[0, jc, rows_g(g), :], None, wbias, win[g]) for g in range(2)]
        kwx = jnp.concatenate([kw_ref[0, pl.ds(r0, t), :], jnp.zeros((t, LANES), BF16)], axis=1)
        both = [_dot_nt(jnp.concatenate([sstreams[g][0], kwx], axis=0), qx[g]) for g in range(2)]
        res = attend(sstreams + wstreams, [both[0][:t], both[1][:t], both[0][t:], both[1][t:]])
        slc, win = res[:2], res[2:]

    def slc_body(it, state):
        j = i - it
        return tuple(attend(slc_streams(j, j, state, None)))

    slc = lax.fori_loop(n_win, i + 1, slc_body, tuple(slc))

    gates = _sigmoid(gt_ref[0]).T
    for m in range(nh):
        halves = []
        for g in range(2):
            c = g * 3 * nh + m * 3
            sl = slice(m * t, (m + 1) * t)
            o_s = slc[g][2][:, sl] * (1.0 / slc[g][1][:, sl])
            o_w = win[g][2][:, sl] * (1.0 / win[g][1][:, sl])
            halves.append(gates[c:c + 1, :] * o_cmp[g][:, sl] + gates[c + 1:c + 2, :] * o_s + gates[c + 2:c + 3, :] * o_w)
        o_ref[0, :, m * LANES:(m + 1) * LANES] = jnp.concatenate(halves, axis=0).T.astype(o_ref.dtype)


def _nsa_attention(qk, vt, kc2, vct, gt, n_cmp):
    b, s, _ = qk.shape
    t = vt.shape[-1]
    assert WINDOW % t == 0 and t % SEL_BLOCK == 0
    nt = s // t
    kspec = lambda col: pl.BlockSpec((1, s, LANES), lambda bi, i: (bi, 0, col))
    vspec = lambda blk: pl.BlockSpec((1, nt, LANES, t), lambda bi, i: (bi, 0, blk, 0))
    cspec = pl.BlockSpec((1, LANES, LANES), lambda bi, i: (bi, 0, 0))
    return pl.pallas_call(
        functools.partial(_nsa_kernel, t=t, n_cmp=n_cmp),
        out_shape=jax.ShapeDtypeStruct((b, s, NSA_WIDTH), BF16),
        grid=(b, nt),
        in_specs=[pl.BlockSpec((1, t, NSA_WIDTH), lambda bi, i: (bi, i, 0)),
                  kspec(4), vspec(0), kspec(5), vspec(1), cspec, cspec,
                  pl.BlockSpec((1, t, LANES), lambda bi, i: (bi, i, 0))],
        out_specs=pl.BlockSpec((1, t, NSA_WIDTH), lambda bi, i: (bi, i, 0)),
        compiler_params=_cparams("parallel", "arbitrary"),
        name="nsa_attention",
    )(qk, qk, vt, qk, vt, kc2, vct, gt)


def _proj_gmlp_kernel(x_ref, w_ref, lg_ref, lb_ref, ws_ref, bs_ref, qk_ref, cv_ref, gt_ref, vt_ref, mlp_ref):
    xb = x_ref[...].astype(BF16)
    tm = x_ref.shape[0]
    c1 = 2 * GMLP_WIDTH
    uv = _dot(xb, w_ref[:, 0:c1])
    qk_ref[...] = _dot(xb, w_ref[:, c1:c1 + 768]).astype(qk_ref.dtype)
    cv_ref[...] = _dot(xb, w_ref[:, c1 + 768:c1 + 1024])
    gt_ref[...] = _dot(xb, w_ref[:, c1 + 1024:c1 + 1024 + LANES])
    vb = _dot(xb, w_ref[:, c1 + 1024 + LANES:]).astype(vt_ref.dtype)
    for s in range(vt_ref.shape[0]):
        vt_ref[s] = vb[s * KEY_TILE:(s + 1) * KEY_TILE, :].astype(F32).T.astype(vt_ref.dtype)
    cg = GMLP_WIDTH // GMLP_GROUPS
    u = _gelu_tanh(uv[:, :GMLP_WIDTH])
    v = _layer_norm(_gelu_tanh(uv[:, GMLP_WIDTH:]), lg_ref[...], lb_ref[...]).astype(BF16)
    rr = lax.broadcasted_iota(jnp.int32, (GMLP_CHUNK, GMLP_CHUNK), 0)
    cc = lax.broadcasted_iota(jnp.int32, (GMLP_CHUNK, GMLP_CHUNK), 1)
    for g in range(GMLP_GROUPS):
        wm = jnp.where(rr >= cc, ws_ref[g], 0.0).astype(BF16)
        bias = bs_ref[:, g:g + 1]
        for c0 in range(0, tm, GMLP_CHUNK):
            mixed = _dot(wm, v[c0:c0 + GMLP_CHUNK, g * cg:(g + 1) * cg]) + bias
            mlp_ref[c0:c0 + GMLP_CHUNK, g * cg:(g + 1) * cg] = (
                u[c0:c0 + GMLP_CHUNK, g * cg:(g + 1) * cg] * mixed).astype(mlp_ref.dtype)


def _proj_gmlp(x, w, ln_g, ln_b, ws, bs, tm=1024):
    m, k = x.shape
    assert tm % GMLP_CHUNK == 0
    vec = lambda a: a.reshape(1, GMLP_WIDTH).astype(F32)
    full = lambda a: pl.BlockSpec(a.shape, lambda i: (0,) * a.ndim)
    row = lambda wd: pl.BlockSpec((tm, wd), lambda i: (i, 0))
    args = (x, w, vec(ln_g), vec(ln_b), ws.astype(F32), jnp.transpose(bs).astype(F32))
    return pl.pallas_call(
        _proj_gmlp_kernel,
        out_shape=[jax.ShapeDtypeStruct((m, 768), BF16), jax.ShapeDtypeStruct((m, 2 * KV_WIDTH), F32),
                   jax.ShapeDtypeStruct((m, LANES), F32),
                   jax.ShapeDtypeStruct((m // KEY_TILE, 2 * KV_WIDTH, KEY_TILE), BF16),
                   jax.ShapeDtypeStruct((m, GMLP_WIDTH), BF16)],
        grid=(m // tm,),
        in_specs=[row(k)] + [full(a) for a in args[1:]],
        out_specs=[row(768), row(2 * KV_WIDTH), row(LANES),
                   pl.BlockSpec((tm // KEY_TILE, 2 * KV_WIDTH, KEY_TILE), lambda i: (i, 0, 0)), row(GMLP_WIDTH)],
        compiler_params=_cparams("parallel"),
        name="proj_gmlp",
    )(*args)


def _head_pair_perm():
    m, g, d = np.meshgrid(np.arange(4), np.arange(2), np.arange(HEAD_DIM), indexing="ij")
    return ((g * 4 + m) * HEAD_DIM + d).reshape(-1)


def _even_layer(x, w_in, conv_w, conv_b, conv_ln_g, conv_ln_b, w_out, post, casts, b, s):
    w_sc = jnp.concatenate([w_in[:, :SB_WIDTH] * (QK_SCALE * LOG2E), w_in[:, SB_WIDTH:]], axis=1)
    w_sc = jnp.concatenate([w_sc[:, 3 * SB_WIDTH:], w_sc[:, :3 * SB_WIDTH]], axis=1)
    qk, vt, o_cv, *cast = _proj_conv(x, w_sc.astype(BF16), conv_w, conv_b, conv_ln_g, conv_ln_b, s, casts)
    o_sb = _sb_attention(qk.reshape(b, s, -1), vt.reshape(b, s // KEY_TILE, SB_WIDTH, KEY_TILE))
    return _post_mixer(o_sb.reshape(b * s, -1), o_cv, w_out.astype(BF16), x, *post(cast)), cast


def _odd_layer(x, w_in, cmpk_pos, cmpk_w1, cmpk_w2, cmpv_pos, cmpv_w1, cmpv_w2,
               gmlp_ln_g, gmlp_ln_b, gmlp_ws, gmlp_bs, w_out, post, b, s):
    perm = _head_pair_perm()
    o_gt, o_u = NSA_WIDTH + 6 * KV_WIDTH, NSA_WIDTH + 6 * KV_WIDTH + 24
    w_gt = jnp.pad(w_in[:, o_gt:o_u], ((0, 0), (0, LANES - 24)))
    col = lambda n: w_in[:, NSA_WIDTH + n * KV_WIDTH:NSA_WIDTH + (n + 1) * KV_WIDTH]
    w_re = jnp.concatenate([w_in[:, o_u:], w_in[:, perm] * (QK_SCALE * LOG2E), col(2), col(4), col(0), col(1), w_gt,
                            col(3), col(5)], axis=1)
    qk, cv, gt, vt, o_mlp = _proj_gmlp(x, w_re.astype(BF16), gmlp_ln_g, gmlp_ln_b, gmlp_ws, gmlp_bs)
    n_cmp = (s - CMP_BLOCK) // CMP_STRIDE + 1
    kc2, vct = _compress(cv.reshape(b, s, -1), cmpk_pos, cmpv_pos, cmpk_w1, cmpv_w1, cmpk_w2, cmpv_w2)
    o_nsa = _nsa_attention(qk.reshape(b, s, -1), vt.reshape(b, s // KEY_TILE, 2 * KV_WIDTH, KEY_TILE), kc2, vct,
                           gt.reshape(b, s, -1), n_cmp)
    w_out_re = jnp.concatenate([w_out[:NSA_WIDTH][perm], w_out[NSA_WIDTH:]], axis=0)
    return _post_mixer(o_nsa.reshape(b * s, -1), o_mlp, w_out_re.astype(BF16), x, *post)


def kernel(x, ev_w_in, ev_conv_w, ev_conv_b, ev_conv_ln_g, ev_conv_ln_b, ev_w_out, od_w_in, od_cmpk_pos, od_cmpk_w1, od_cmpk_w2, od_cmpv_pos, od_cmpv_w1, od_cmpv_w2, od_gmlp_ln_g, od_gmlp_ln_b, od_gmlp_ws, od_gmlp_bs, od_w_out, ffn_w_gate, ffn_w_up, ffn_w_down, ln1_g, ln1_b, ln2_g, ln2_b):
    b, s, d = x.shape
    assert DEPTH == 2 and s // CMP_STRIDE == LANES and s // SEL_BLOCK == SEL_LANES
    h = x.reshape(b * s, d)
    ff = ffn_w_gate.shape[-1]
    casts = (ffn_w_gate.reshape(DEPTH * d, ff), ffn_w_up.reshape(DEPTH * d, ff), ffn_w_down.reshape(DEPTH * ff, d))
    post = lambda layer, w: (ln1_g[layer], ln1_b[layer], *w, layer, ln2_g[layer], ln2_b[layer])
    h, ffn_w = _even_layer(h, ev_w_in[0], ev_conv_w[0], ev_conv_b[0], ev_conv_ln_g[0], ev_conv_ln_b[0], ev_w_out[0],
                           functools.partial(post, 0), casts, b, s)
    h = _odd_layer(h, od_w_in[0], od_cmpk_pos[0], od_cmpk_w1[0], od_cmpk_w2[0], od_cmpv_pos[0], od_cmpv_w1[0],
                   od_cmpv_w2[0], od_gmlp_ln_g[0], od_gmlp_ln_b[0], od_gmlp_ws[0], od_gmlp_bs[0], od_w_out[0],
                   post(1, ffn_w), b, s)
    return h.reshape(b, s, d)
```

```python
import functools

import numpy as np
import jax
import jax.numpy as jnp
from jax import lax
from jax.experimental import pallas as pl
from jax.experimental.pallas import tpu as pltpu

F32 = jnp.float32
BF16 = jnp.bfloat16

D_MODEL = 1024
DEPTH = 2
HEAD_DIM = 64
LANES = 128
SUBLANES = 8
SB_WIDTH = 512
SB_HEADS = 8
CONV_CH = 512
CONV_WIDTH = 31
NSA_WIDTH = 512
KV_WIDTH = 128
CMP_BLOCK = 32
CMP_STRIDE = 16
SEL_BLOCK = 64
SEL_TOPK = 8
SEL_LANES = 32
WINDOW = 512
GMLP_WIDTH = 512
GMLP_GROUPS = 4
GMLP_CHUNK = 128
D_FF = 2816
ALPHA = (2 * DEPTH) ** 0.25
LN_EPS = 1e-5
NEG = -1e30
FORCE_BONUS = 1e3
QK_SCALE = HEAD_DIM ** -0.5
LOG2E = float(np.log2(np.e))
NSA_BIAS_LANE = SEL_LANES

VMEM_LIMIT = 56 * 1024 * 1024


def _cparams(*sem):
    return pltpu.CompilerParams(dimension_semantics=sem, vmem_limit_bytes=VMEM_LIMIT)


def _layer_norm(y, g, b):
    mu = jnp.mean(y, axis=-1, keepdims=True)
    d = y - mu
    var = jnp.mean(d * d, axis=-1, keepdims=True)
    return d * lax.rsqrt(var + LN_EPS) * g + b


def _sigmoid(x):
    return 1.0 / (1.0 + jnp.exp2(x * -LOG2E))


def _gelu_tanh(x):
    c = np.float32(np.sqrt(2.0 / np.pi))
    return x * (0.5 * (1.0 + jnp.tanh(c * (x + 0.044715 * (x * x * x)))))


def _dot(a, b):
    return jnp.dot(a, b, preferred_element_type=F32)


def _dot_nt(a, b):
    return lax.dot_general(a, b, (((1,), (1,)), ((), ())), preferred_element_type=F32)


KEY_TILE = 256


SB_UNDERFLOW = 150.0


def _sb_kernel(q_ref, k_ref, vt_ref, o_ref, *, t, nh):
    i = pl.program_id(2)
    lane = lax.broadcasted_iota(jnp.int32, (1, LANES), 1)
    rr = lax.broadcasted_iota(jnp.int32, (t, t), 0)
    cc = lax.broadcasted_iota(jnp.int32, (t, t), 1)
    before = rr < cc
    later = jnp.where(before, 1.0, 0.0).astype(BF16)
    hs = range(nh)
    pair = lambda h: slice((h // 2) * LANES, (h // 2 + 1) * LANES)
    qh = [q_ref[0, :, pair(h)] * jnp.where(lane // HEAD_DIM == h % 2, 1.0, 0.0).astype(BF16) for h in hs]

    def tile(j, state, diag):
        r0 = pl.multiple_of(j * t, t)
        z = [_dot_nt(k_ref[0, pl.ds(r0, t), pair(h)], qh[h]) for h in hs]
        nz = [-z[h] for h in hs]
        lg = [jnp.log2(1.0 + jnp.exp2(jnp.minimum(z[h], nz[h]))) for h in hs]
        lb = [jnp.minimum(nz[h], 0.0) - lg[h] for h in hs]
        lr = [jnp.where(before, lb[h], 0.0) for h in hs] if diag else lb
        hi = [lr[h].astype(BF16) for h in hs]
        hl = [jnp.concatenate([hi[h], (lr[h] - hi[h].astype(F32)).astype(BF16)], axis=1) for h in hs]
        bt = [_dot(later, hl[h]) for h in hs]
        w = [jnp.exp2(lb[h] + z[h] + (bt[h][:, :t] + bt[h][:, t:] + state[h][0])) for h in hs]
        if diag:
            w = [jnp.where(before, w[h], 0.0) for h in hs]
        pv = [_dot(vt_ref[0, j, h * HEAD_DIM:(h + 1) * HEAD_DIM, :], w[h].astype(BF16)) for h in hs]
        return tuple((state[h][0] + jnp.sum(lr[h], axis=0, keepdims=True), state[h][1] + pv[h]) for h in hs)

    zero = (jnp.zeros((1, t), F32), jnp.zeros((HEAD_DIM, t), F32))
    state = tile(i, (zero,) * nh, True)

    def cond(c):
        return (c[0] >= 0) & (c[1] > -SB_UNDERFLOW)

    def body(c):
        st = tile(c[0], c[2], False)
        top = st[0][0]
        for h in range(1, nh):
            top = jnp.maximum(top, st[h][0])
        return c[0] - 1, jnp.max(top), st

    _, _, state = lax.while_loop(cond, body, (i - 1, jnp.float32(0.0), state))
    for p in range(nh // 2):
        o_ref[0, :, p * LANES:(p + 1) * LANES] = jnp.concatenate(
            [state[2 * p][1], state[2 * p + 1][1]], axis=0).T.astype(o_ref.dtype)


def _sb_attention(qk, vt, nh=SB_HEADS):
    b, s, _ = qk.shape
    t = vt.shape[-1]
    wb = nh * HEAD_DIM
    nblk = SB_WIDTH // wb
    return pl.pallas_call(
        functools.partial(_sb_kernel, t=t, nh=nh),
        out_shape=jax.ShapeDtypeStruct((b, s, SB_WIDTH), BF16),
        grid=(b, nblk, s // t),
        in_specs=[pl.BlockSpec((1, t, wb), lambda bi, hb, i: (bi, i, hb)),
                  pl.BlockSpec((1, s, wb), lambda bi, hb, i: (bi, 0, nblk + hb)),
                  pl.BlockSpec((1, s // t, wb, t), lambda bi, hb, i: (bi, 0, hb, 0))],
        out_specs=pl.BlockSpec((1, t, wb), lambda bi, hb, i: (bi, i, hb)),
        compiler_params=_cparams("parallel", "parallel", "arbitrary"),
        name="sb_attention",
    )(qk, qk, vt)


CONV_PAD = 32


def _proj_conv_kernel(x_ref, w_ref, cw_ref, cb_ref, clg_ref, clb_ref, *rest, tiles_per_seq, sub, n_cast):
    casts_in, (qk_ref, vt_ref, cv_ref), casts_out, hist = (
        rest[:n_cast], rest[n_cast:n_cast + 3], rest[n_cast + 3:2 * n_cast + 3], rest[-1])
    for src_ref, dst_ref in zip(casts_in, casts_out):
        dst_ref[...] = src_ref[...].astype(dst_ref.dtype)
    i = pl.program_id(0)
    tm = x_ref.shape[0]
    xb = x_ref[...].astype(BF16)
    off = CONV_PAD - (CONV_WIDTH - 1)
    prev = jnp.where(i % tiles_per_seq == 0, 0.0, hist[...])
    for r0 in range(0, tm, sub):
        ag = _dot(xb[r0:r0 + sub, :], w_ref[:, 0:2 * CONV_CH])
        if r0 == 0:
            qk_ref[...] = _dot(xb, w_ref[:, 2 * CONV_CH:2 * CONV_CH + 2 * SB_WIDTH]).astype(qk_ref.dtype)
        if r0 == sub:
            vb = _dot(xb, w_ref[:, 2 * CONV_CH + 2 * SB_WIDTH:]).astype(vt_ref.dtype)
            for s in range(vt_ref.shape[0]):
                vt_ref[s] = vb[s * KEY_TILE:(s + 1) * KEY_TILE, :].astype(F32).T.astype(vt_ref.dtype)
        glu = ag[:, :CONV_CH] * _sigmoid(ag[:, CONV_CH:])
        hext = jnp.concatenate([prev, glu], axis=0)
        prev = glu[sub - CONV_PAD:, :]
        acc = jnp.zeros((sub, CONV_CH), F32) + cb_ref[...]
        for ph in range(SUBLANES):
            taps = [w for w in range(CONV_WIDTH) if (off + w) % SUBLANES == ph]
            rows = sub + (SUBLANES if ph else 0)
            part = None
            for w in taps:
                a0 = (off + w) // SUBLANES * SUBLANES
                term = cw_ref[w:w + 1, :] * hext[a0:a0 + rows, :]
                part = term if part is None else part + term
            acc = acc + part[ph:ph + sub, :]
        y = _layer_norm(acc, clg_ref[...], clb_ref[...])
        cv_ref[r0:r0 + sub, :] = (y * _sigmoid(y)).astype(cv_ref.dtype)
    hist[...] = prev


def _proj_conv(x, w, conv_w, conv_b, conv_ln_g, conv_ln_b, seq, casts=(), tm=1024, sub=128):
    m, k = x.shape
    n = w.shape[1]
    steps = m // tm
    assert seq % tm == 0 and tm % sub == 0 and tm >= 2 * sub and sub >= CONV_PAD
    assert all(c.shape[0] % (16 * steps) == 0 for c in casts)
    cvec = lambda v: v.reshape(1, CONV_CH).astype(F32)
    full = lambda shape: pl.BlockSpec(shape, lambda i: (0, 0))
    slab = lambda c: pl.BlockSpec((c.shape[0] // steps, c.shape[1]), lambda i: (i, 0))
    return pl.pallas_call(
        functools.partial(_proj_conv_kernel, tiles_per_seq=seq // tm, sub=sub, n_cast=len(casts)),
        out_shape=[jax.ShapeDtypeStruct((m, 2 * SB_WIDTH), BF16),
                   jax.ShapeDtypeStruct((m // KEY_TILE, SB_WIDTH, KEY_TILE), BF16),
                   jax.ShapeDtypeStruct((m, CONV_CH), BF16)] + [jax.ShapeDtypeStruct(c.shape, BF16) for c in casts],
        grid=(steps,),
        in_specs=[pl.BlockSpec((tm, k), lambda i: (i, 0)), full((k, n)), full((CONV_WIDTH, CONV_CH)),
                  full((1, CONV_CH)), full((1, CONV_CH)), full((1, CONV_CH))] + [slab(c) for c in casts],
        out_specs=[pl.BlockSpec((tm, 2 * SB_WIDTH), lambda i: (i, 0)),
                   pl.BlockSpec((tm // KEY_TILE, SB_WIDTH, KEY_TILE), lambda i: (i, 0, 0)),
                   pl.BlockSpec((tm, CONV_CH), lambda i: (i, 0))] + [slab(c) for c in casts],
        scratch_shapes=[pltpu.VMEM((CONV_PAD, CONV_CH), F32)],
        compiler_params=_cparams("arbitrary"),
        name="proj_conv",
    )(x, w, conv_w.astype(F32), cvec(conv_b), cvec(conv_ln_g), cvec(conv_ln_b), *casts)


FF_CHUNK = 256


def _post_mixer_kernel(l_ref, r_ref, wo_ref, x_ref, g1_ref, b1_ref, wg_ref, wu_ref, wd_ref, g2_ref, b2_ref, o_ref):
    tm, kl = l_ref.shape
    hm = tm // 2
    halves = (slice(0, hm), slice(hm, tm))
    n_chunks = D_FF // FF_CHUNK
    chunk = lambda n: slice(n * FF_CHUNK, (n + 1) * FF_CHUNK)
    m = [_dot(l_ref[rows, :], wo_ref[0:kl, :]) + _dot(r_ref[rows, :], wo_ref[kl:, :]) for rows in halves]
    x, xb, gate, up = [None, None], [None, None], [None, None], [None, None]
    for p, rows in enumerate(halves):
        x[p] = _layer_norm(ALPHA * x_ref[rows, :] + m[p], g1_ref[...], b1_ref[...])
        xb[p] = x[p].astype(BF16)
        gate[p] = _dot(xb[p], wg_ref[:, chunk(0)])
        up[p] = _dot(xb[p], wu_ref[:, chunk(0)])
    xb = jnp.concatenate(xb, axis=0)
    gate = jnp.concatenate(gate, axis=0)
    up = jnp.concatenate(up, axis=0)
    acc = jnp.zeros((tm, x_ref.shape[1]), F32)
    for n in range(n_chunks):
        if n > 0:
            gate = _dot(xb, wg_ref[:, chunk(n)])
            up = _dot(xb, wu_ref[:, chunk(n)])
        h = (gate * _sigmoid(gate) * up).astype(BF16)
        if n < n_chunks - 1:
            acc = acc + _dot(h, wd_ref[chunk(n), :])
    for p, rows in enumerate(halves):
        y = acc[rows, :] + _dot(h[rows, :], wd_ref[chunk(n_chunks - 1), :])
        o_ref[rows, :] = _layer_norm(ALPHA * x[p] + y, g2_ref[...], b2_ref[...])


def _post_mixer(left, right, wo, x, g1, b1, wg, wu, wd, layer, g2, b2, tm=512):
    m, d = x.shape
    kl, kr = left.shape[1], right.shape[1]
    ff = wg.shape[1]
    row = lambda w: pl.BlockSpec((tm, w), lambda i: (i, 0))
    res = lambda shape, blk=0: pl.BlockSpec(shape, lambda i: (blk, 0), pipeline_mode=pl.Buffered(1))
    vec = lambda v: v.reshape(1, d).astype(F32)
    return pl.pallas_call(
        _post_mixer_kernel,
        out_shape=jax.ShapeDtypeStruct((m, d), F32),
        grid=(m // tm,),
        in_specs=[row(kl), row(kr), res((kl + kr, d)), row(d), res((1, d)), res((1, d)),
                  res((d, ff), layer), res((d, ff), layer), res((ff, d), layer), res((1, d)), res((1, d))],
        out_specs=row(d),
        compiler_params=_cparams("parallel"),
        name="post_mixer",
    )(left, right, wo, x, vec(g1), vec(b1), wg, wu, wd, vec(g2), vec(b2))


def _compress_kernel(xk_ref, xv_ref, pos_ref, w1_ref, w2_ref, kc_ref, vct_ref):
    n = xk_ref.shape[1] // CMP_STRIDE
    out = []
    for s, x_ref in enumerate((xk_ref, xv_ref)):
        xs = [x_ref[0, pl.ds(r, n, stride=CMP_STRIDE), :] for r in range(CMP_STRIDE)]
        half = lambda o: jnp.concatenate([x + pos_ref[s, o + r:o + r + 1, :] for r, x in enumerate(xs)], axis=1)
        top = _dot(half(0).astype(BF16), w1_ref[s, 0])
        bot = _dot(half(CMP_STRIDE).astype(BF16), w1_ref[s, 1])
        h = top + pltpu.roll(bot, n - 1, 0)
        out.append(_dot(_gelu_tanh(h).astype(BF16), w2_ref[s]))
    kc_ref[0] = out[0].astype(kc_ref.dtype)
    vct_ref[0] = out[1].astype(vct_ref.dtype).astype(F32).T.astype(vct_ref.dtype)


def _block_diag2(a):
    z = jnp.zeros_like(a)
    return jnp.concatenate([jnp.concatenate([a, z], axis=-1), jnp.concatenate([z, a], axis=-1)], axis=-2)


def _compress(cv, pos_k, pos_v, w1k, w1v, w2k, w2v):
    b, s, _ = cv.shape
    n = s // CMP_STRIDE
    pos = jnp.stack([jnp.concatenate([p, p], axis=-1) for p in (pos_k, pos_v)]).astype(F32)
    w1 = jnp.stack([_block_diag2(w.reshape(CMP_BLOCK, HEAD_DIM, HEAD_DIM)) for w in (w1k, w1v)])
    w1 = w1.reshape(2, 2, CMP_STRIDE * KV_WIDTH, KV_WIDTH).astype(BF16)
    w2 = jnp.stack([_block_diag2(w) for w in (w2k, w2v)]).astype(BF16)
    full = lambda a: pl.BlockSpec(a.shape, lambda bi: (0,) * a.ndim)
    return pl.pallas_call(
        _compress_kernel,
        out_shape=[jax.ShapeDtypeStruct((b, n, KV_WIDTH), BF16), jax.ShapeDtypeStruct((b, KV_WIDTH, n), BF16)],
        grid=(b,),
        in_specs=[pl.BlockSpec((1, s, KV_WIDTH), lambda bi: (bi, 0, 0)),
                  pl.BlockSpec((1, s, KV_WIDTH), lambda bi: (bi, 0, 1)), full(pos), full(w1), full(w2)],
        out_specs=[pl.BlockSpec((1, n, KV_WIDTH), lambda bi: (bi, 0, 0)),
                   pl.BlockSpec((1, KV_WIDTH, n), lambda bi: (bi, 0, 0))],
        compiler_params=_cparams("parallel"),
        name="nsa_compress",
    )(cv, cv, pos, w1, w2)


NSA_GROUP = 4


def _nsa_kernel(q_ref, ks_ref, vst_ref, kw_ref, vwt_ref, kc_ref, vct_ref, gt_ref, o_ref, *, t, n_cmp):
    i = pl.program_id(1)
    nh = NSA_GROUP
    lane = lax.broadcasted_iota(jnp.int32, (1, LANES), 1)
    tpos = i * t + lax.broadcasted_iota(jnp.int32, (1, t), 1)
    krow = lax.broadcasted_iota(jnp.int32, (t, 1), 0)
    qg = []
    for g in range(2):
        hm = jnp.where(lane // HEAD_DIM == g, 1.0, 0.0).astype(BF16)
        qg.append(jnp.concatenate([q_ref[0, :, m * LANES:(m + 1) * LANES] * hm for m in range(nh)], axis=0))

    nblk = lax.broadcasted_iota(jnp.int32, (LANES, 1), 0)
    cvalid = (nblk * CMP_STRIDE + (CMP_BLOCK - 1) <= tpos) & (nblk < n_cmp)
    jr = lax.broadcasted_iota(jnp.int32, (SEL_LANES, LANES), 0)
    nc = lax.broadcasted_iota(jnp.int32, (SEL_LANES, LANES), 1)
    ovl_t = jnp.where((nc * CMP_STRIDE < jr * SEL_BLOCK + SEL_BLOCK) & (nc * CMP_STRIDE + CMP_BLOCK > jr * SEL_BLOCK)
                      & (nc < n_cmp), 1.0, 0.0).astype(BF16)
    jsel = lax.broadcasted_iota(jnp.int32, (SEL_LANES, 1), 0)
    blk_t = tpos // SEL_BLOCK
    forced = (jsel == 0) | (jsel == blk_t) | (jsel == blk_t - 1)
    pad_rows = lax.broadcasted_iota(jnp.int32, (LANES - SEL_LANES, 1), 0)
    o_cmp = []
    qx = []
    for g in range(2):
        st = _dot_nt(kc_ref[0], qg[g])
        ps = []
        psum = jnp.zeros((LANES, t), F32)
        for m in range(nh):
            s = jnp.where(cvalid, st[:, m * t:(m + 1) * t], NEG)
            e = jnp.where(cvalid, jnp.exp2(s - jnp.max(s, axis=0, keepdims=True)), 0.0)
            l = jnp.sum(e, axis=0, keepdims=True)
            p = e * (1.0 / jnp.where(l > 0.0, l, 1.0))
            ps.append(p.astype(BF16))
            psum = psum + p
        o_cmp.append(_dot(vct_ref[0, g * HEAD_DIM:(g + 1) * HEAD_DIM, :], jnp.concatenate(ps, axis=1)))
        hi = psum.astype(BF16)
        lo = (psum - hi.astype(F32)).astype(BF16)
        it = _dot(ovl_t, jnp.concatenate([hi, lo], axis=1))
        imp = it[:, :t] + it[:, t:]
        score = jnp.where(jsel <= blk_t, imp + jnp.where(forced, FORCE_BONUS, 0.0), NEG)
        cnt = jnp.zeros((SEL_LANES, t), F32)
        for c in range(SEL_LANES):
            row = score[c:c + 1, :]
            cnt = cnt + jnp.where((row > score) | ((row == score) & (jsel > c)), 1.0, 0.0)
        tail = jnp.where(pad_rows == NSA_BIAS_LANE - SEL_LANES, NEG, 0.0) + jnp.zeros((1, t), F32)
        selt = jnp.concatenate([jnp.where(cnt < SEL_TOPK, 0.0, NEG), tail], axis=0).T.astype(BF16)
        qx.append(jnp.concatenate([qg[g], jnp.concatenate([selt] * nh, axis=0)], axis=1))

    def attend(streams, scores=None):
        if scores is None:
            scores = [_dot_nt(k, q) for k, _, q, _, _ in streams]
        sb = [s if bias is None else s + jnp.concatenate([bias] * nh, axis=1)
              for s, (_, _, _, bias, _) in zip(scores, streams)]
        mnew = [jnp.maximum(st[0], jnp.max(s, axis=0, keepdims=True)) for s, (_, _, _, _, st) in zip(sb, streams)]
        p = [jnp.exp2(s - mn) for s, mn in zip(sb, mnew)]
        a = [jnp.exp2(st[0] - mn) for mn, (_, _, _, _, st) in zip(mnew, streams)]
        pv = [_dot(vt, pp.astype(BF16)) for pp, (_, vt, _, _, _) in zip(p, streams)]
        return [(mn, aa * st[1] + jnp.sum(pp, axis=0, keepdims=True), aa * st[2] + o)
                for mn, aa, pp, o, (_, _, _, _, st) in zip(mnew, a, p, pv, streams)]

    init = (jnp.full((1, nh * t), NEG, F32), jnp.zeros((1, nh * t), F32), jnp.zeros((HEAD_DIM, nh * t), F32))
    per_tile = t // SEL_BLOCK
    rows_g = lambda g: slice(g * HEAD_DIM, (g + 1) * HEAD_DIM)

    def slc_streams(j, jc, state, bias):
        onehot = (lane == j * per_tile + krow // SEL_BLOCK) | ((lane == NSA_BIAS_LANE) & (j < 0))
        r0 = pl.multiple_of(jc * t, t)
        kx = jnp.concatenate([ks_ref[0, pl.ds(r0, t), :], jnp.where(onehot, 1.0, 0.0).astype(BF16)], axis=1)
        return [(kx, vst_ref[0, jc, rows_g(g), :], qx[g], bias, state[g]) for g in range(2)]

    slc = [init, init]
    win = [init, init]
    n_win = WINDOW // t + 1
    for d in range(n_win):
        j = i - d
        jc = jnp.maximum(j, 0)
        kpos = j * t + krow
        wbias = jnp.where((kpos >= 0) & (kpos <= tpos) & (tpos - kpos < WINDOW), 0.0, NEG)
        causal = jnp.where(kpos <= tpos, 0.0, NEG) if d == 0 else None
        r0 = pl.multiple_of(jc * t, t)
        sstreams = slc_streams(j, jc, slc, causal)
        wstreams = [(None, vwt_ref[0, jc, rows_g(g), :], None, wbias, win[g]) for g in range(2)]
        kwx = jnp.concatenate([kw_ref[0, pl.ds(r0, t), :], jnp.zeros((t, LANES), BF16)], axis=1)
        both = [_dot_nt(jnp.concatenate([sstreams[g][0], kwx], axis=0), qx[g]) for g in range(2)]
        res = attend(sstreams + wstreams, [both[0][:t], both[1][:t], both[0][t:], both[1][t:]])
        slc, win = res[:2], res[2:]

    def slc_body(it, state):
        j = i - it
        return tuple(attend(slc_streams(j, j, state, None)))

    slc = lax.fori_loop(n_win, i + 1, slc_body, tuple(slc))

    gates = _sigmoid(gt_ref[0]).T
    for m in range(nh):
        halves = []
        for g in range(2):
            c = g * 3 * nh + m * 3
            sl = slice(m * t, (m + 1) * t)
            o_s = slc[g][2][:, sl] * (1.0 / slc[g][1][:, sl])
            o_w = win[g][2][:, sl] * (1.0 / win[g][1][:, sl])
            halves.append(gates[c:c + 1, :] * o_cmp[g][:, sl] + gates[c + 1:c + 2, :] * o_s + gates[c + 2:c + 3, :] * o_w)
        o_ref[0, :, m * LANES:(m + 1) * LANES] = jnp.concatenate(halves, axis=0).T.astype(o_ref.dtype)


def _nsa_attention(qk, vt, kc2, vct, gt, n_cmp):
    b, s, _ = qk.shape
    t = vt.shape[-1]
    assert WINDOW % t == 0 and t % SEL_BLOCK == 0
    nt = s // t
    kspec = lambda col: pl.BlockSpec((1, s, LANES), lambda bi, i: (bi, 0, col))
    vspec = lambda blk: pl.BlockSpec((1, nt, LANES, t), lambda bi, i: (bi, 0, blk, 0))
    cspec = pl.BlockSpec((1, LANES, LANES), lambda bi, i: (bi, 0, 0))
    return pl.pallas_call(
        functools.partial(_nsa_kernel, t=t, n_cmp=n_cmp),
        out_shape=jax.ShapeDtypeStruct((b, s, NSA_WIDTH), BF16),
        grid=(b, nt),
        in_specs=[pl.BlockSpec((1, t, NSA_WIDTH), lambda bi, i: (bi, i, 0)),
                  kspec(4), vspec(0), kspec(5), vspec(1), cspec, cspec,
                  pl.BlockSpec((1, t, LANES), lambda bi, i: (bi, i, 0))],
        out_specs=pl.BlockSpec((1, t, NSA_WIDTH), lambda bi, i: (bi, i, 0)),
        compiler_params=_cparams("parallel", "arbitrary"),
        name="nsa_attention",
    )(qk, qk, vt, qk, vt, kc2, vct, gt)


def _proj_gmlp_kernel(x_ref, w_ref, lg_ref, lb_ref, ws_ref, bs_ref, qk_ref, cv_ref, gt_ref, vt_ref, mlp_ref):
    xb = x_ref[...].astype(BF16)
    tm = x_ref.shape[0]
    c1 = 2 * GMLP_WIDTH
    uv = _dot(xb, w_ref[:, 0:c1])
    qk_ref[...] = _dot(xb, w_ref[:, c1:c1 + 768]).astype(qk_ref.dtype)
    cv_ref[...] = _dot(xb, w_ref[:, c1 + 768:c1 + 1024])
    gt_ref[...] = _dot(xb, w_ref[:, c1 + 1024:c1 + 1024 + LANES])
    vb = _dot(xb, w_ref[:, c1 + 1024 + LANES:]).astype(vt_ref.dtype)
    for s in range(vt_ref.shape[0]):
        vt_ref[s] = vb[s * KEY_TILE:(s + 1) * KEY_TILE, :].astype(F32).T.astype(vt_ref.dtype)
    cg = GMLP_WIDTH // GMLP_GROUPS
    u = _gelu_tanh(uv[:, :GMLP_WIDTH])
    v = _layer_norm(_gelu_tanh(uv[:, GMLP_WIDTH:]), lg_ref[...], lb_ref[...]).astype(BF16)
    rr = lax.broadcasted_iota(jnp.int32, (GMLP_CHUNK, GMLP_CHUNK), 0)
    cc = lax.broadcasted_iota(jnp.int32, (GMLP_CHUNK, GMLP_CHUNK), 1)
    for g in range(GMLP_GROUPS):
        wm = jnp.where(rr >= cc, ws_ref[g], 0.0).astype(BF16)
        bias = bs_ref[:, g:g + 1]
        for c0 in range(0, tm, GMLP_CHUNK):
            mixed = _dot(wm, v[c0:c0 + GMLP_CHUNK, g * cg:(g + 1) * cg]) + bias
            mlp_ref[c0:c0 + GMLP_CHUNK, g * cg:(g + 1) * cg] = (
                u[c0:c0 + GMLP_CHUNK, g * cg:(g + 1) * cg] * mixed).astype(mlp_ref.dtype)


def _proj_gmlp(x, w, ln_g, ln_b, ws, bs, tm=1024):
    m, k = x.shape
    assert tm % GMLP_CHUNK == 0
    vec = lambda a: a.reshape(1, GMLP_WIDTH).astype(F32)
    full = lambda a: pl.BlockSpec(a.shape, lambda i: (0,) * a.ndim)
    row = lambda wd: pl.BlockSpec((tm, wd), lambda i: (i, 0))
    args = (x, w, vec(ln_g), vec(ln_b), ws.astype(F32), jnp.transpose(bs).astype(F32))
    return pl.pallas_call(
        _proj_gmlp_kernel,
        out_shape=[jax.ShapeDtypeStruct((m, 768), BF16), jax.ShapeDtypeStruct((m, 2 * KV_WIDTH), F32),
                   jax.ShapeDtypeStruct((m, LANES), F32),
                   jax.ShapeDtypeStruct((m // KEY_TILE, 2 * KV_WIDTH, KEY_TILE), BF16),
                   jax.ShapeDtypeStruct((m, GMLP_WIDTH), BF16)],
        grid=(m // tm,),
        in_specs=[row(k)] + [full(a) for a in args[1:]],
        out_specs=[row(768), row(2 * KV_WIDTH), row(LANES),
                   pl.BlockSpec((tm // KEY_TILE, 2 * KV_WIDTH, KEY_TILE), lambda i: (i, 0, 0)), row(GMLP_WIDTH)],
        compiler_params=_cparams("parallel"),
        name="proj_gmlp",
    )(*args)


def _head_pair_perm():
    m, g, d = np.meshgrid(np.arange(4), np.arange(2), np.arange(HEAD_DIM), indexing="ij")
    return ((g * 4 + m) * HEAD_DIM + d).reshape(-1)


def _even_layer(x, w_in, conv_w, conv_b, conv_ln_g, conv_ln_b, w_out, post, casts, b, s):
    w_sc = jnp.concatenate([w_in[:, :SB_WIDTH] * (QK_SCALE * LOG2E), w_in[:, SB_WIDTH:]], axis=1)
    w_sc = jnp.concatenate([w_sc[:, 3 * SB_WIDTH:], w_sc[:, :3 * SB_WIDTH]], axis=1)
    qk, vt, o_cv, *cast = _proj_conv(x, w_sc.astype(BF16), conv_w, conv_b, conv_ln_g, conv_ln_b, s, casts)
    o_sb = _sb_attention(qk.reshape(b, s, -1), vt.reshape(b, s // KEY_TILE, SB_WIDTH, KEY_TILE))
    return _post_mixer(o_sb.reshape(b * s, -1), o_cv, w_out.astype(BF16), x, *post(cast)), cast


def _odd_layer(x, w_in, cmpk_pos, cmpk_w1, cmpk_w2, cmpv_pos, cmpv_w1, cmpv_w2,
               gmlp_ln_g, gmlp_ln_b, gmlp_ws, gmlp_bs, w_out, post, b, s):
    perm = _head_pair_perm()
    o_gt, o_u = NSA_WIDTH + 6 * KV_WIDTH, NSA_WIDTH + 6 * KV_WIDTH + 24
    w_gt = jnp.pad(w_in[:, o_gt:o_u], ((0, 0), (0, LANES - 24)))
    col = lambda n: w_in[:, NSA_WIDTH + n * KV_WIDTH:NSA_WIDTH + (n + 1) * KV_WIDTH]
    w_re = jnp.concatenate([w_in[:, o_u:], w_in[:, perm] * (QK_SCALE * LOG2E), col(2), col(4), col(0), col(1), w_gt,
                            col(3), col(5)], axis=1)
    qk, cv, gt, vt, o_mlp = _proj_gmlp(x, w_re.astype(BF16), gmlp_ln_g, gmlp_ln_b, gmlp_ws, gmlp_bs)
    n_cmp = (s - CMP_BLOCK) // CMP_STRIDE + 1
    kc2, vct = _compress(cv.reshape(b, s, -1), cmpk_pos, cmpv_pos, cmpk_w1, cmpv_w1, cmpk_w2, cmpv_w2)
    o_nsa = _nsa_attention(qk.reshape(b, s, -1), vt.reshape(b, s // KEY_TILE, 2 * KV_WIDTH, KEY_TILE), kc2, vct,
                           gt.reshape(b, s, -1), n_cmp)
    w_out_re = jnp.concatenate([w_out[:NSA_WIDTH][perm], w_out[NSA_WIDTH:]], axis=0)
    return _post_mixer(o_nsa.reshape(b * s, -1), o_mlp, w_out_re.astype(BF16), x, *post)


def kernel(x, ev_w_in, ev_conv_w, ev_conv_b, ev_conv_ln_g, ev_conv_ln_b, ev_w_out, od_w_in, od_cmpk_pos, od_cmpk_w1, od_cmpk_w2, od_cmpv_pos, od_cmpv_w1, od_cmpv_w2, od_gmlp_ln_g, od_gmlp_ln_b, od_gmlp_ws, od_gmlp_bs, od_w_out, ffn_w_gate, ffn_w_up, ffn_w_down, ln1_g, ln1_b, ln2_g, ln2_b):
    b, s, d = x.shape
    assert DEPTH == 2 and s // CMP_STRIDE == LANES and s // SEL_BLOCK == SEL_LANES
    h = x.reshape(b * s, d)
    ff = ffn_w_gate.shape[-1]
    casts = (ffn_w_gate.reshape(DEPTH * d, ff), ffn_w_up.reshape(DEPTH * d, ff), ffn_w_down.reshape(DEPTH * ff, d))
    post = lambda layer, w: (ln1_g[layer], ln1_b[layer], *w, layer, ln2_g[layer], ln2_b[layer])
    h, ffn_w = _even_layer(h, ev_w_in[0], ev_conv_w[0], ev_conv_b[0], ev_conv_ln_g[0], ev_conv_ln_b[0], ev_w_out[0],
                           functools.partial(post, 0), casts, b, s)
    h = _odd_layer(h, od_w_in[0], od_cmpk_pos[0], od_cmpk_w1[0], od_cmpk_w2[0], od_cmpv_pos[0], od_cmpv_w1[0],
                   od_cmpv_w2[0], od_gmlp_ln_g[0], od_gmlp_ln_b[0], od_gmlp_ws[0], od_gmlp_bs[0], od_w_out[0],
                   post(1, ffn_w), b, s)
    return h.reshape(b, s, d)
```

```python
import functools

import numpy as np
import jax
import jax.numpy as jnp
from jax import lax
from jax.experimental import pallas as pl
from jax.experimental.pallas import tpu as pltpu

F32 = jnp.float32
BF16 = jnp.bfloat16

D_MODEL = 1024
DEPTH = 2
HEAD_DIM = 64
LANES = 128
SUBLANES = 8
SB_WIDTH = 512
SB_HEADS = 8
CONV_CH = 512
CONV_WIDTH = 31
NSA_WIDTH = 512
KV_WIDTH = 128
CMP_BLOCK = 32
CMP_STRIDE = 16
SEL_BLOCK = 64
SEL_TOPK = 8
SEL_LANES = 32
WINDOW = 512
GMLP_WIDTH = 512
GMLP_GROUPS = 4
GMLP_CHUNK = 128
D_FF = 2816
ALPHA = (2 * DEPTH) ** 0.25
LN_EPS = 1e-5
NEG = -1e30
FORCE_BONUS = 1e3
QK_SCALE = HEAD_DIM ** -0.5
LOG2E = float(np.log2(np.e))
NSA_BIAS_LANE = SEL_LANES

VMEM_LIMIT = 56 * 1024 * 1024


def _cparams(*sem):
    return pltpu.CompilerParams(dimension_semantics=sem, vmem_limit_bytes=VMEM_LIMIT)


def _layer_norm(y, g, b):
    mu = jnp.mean(y, axis=-1, keepdims=True)
    d = y - mu
    var = jnp.mean(d * d, axis=-1, keepdims=True)
    return d * lax.rsqrt(var + LN_EPS) * g + b


def _sigmoid(x):
    return 1.0 / (1.0 + jnp.exp2(x * -LOG2E))


def _gelu_tanh(x):
    c = np.float32(np.sqrt(2.0 / np.pi))
    return x * (0.5 * (1.0 + jnp.tanh(c * (x + 0.044715 * (x * x * x)))))


def _dot(a, b):
    return jnp.dot(a, b, preferred_element_type=F32)


def _dot_nt(a, b):
    return lax.dot_general(a, b, (((1,), (1,)), ((), ())), preferred_element_type=F32)


KEY_TILE = 256


SB_UNDERFLOW = 150.0


def _sb_kernel(q_ref, k_ref, vt_ref, o_ref, *, t, nh):
    i = pl.program_id(2)
    lane = lax.broadcasted_iota(jnp.int32, (1, LANES), 1)
    rr = lax.broadcasted_iota(jnp.int32, (t, t), 0)
    cc = lax.broadcasted_iota(jnp.int32, (t, t), 1)
    before = rr < cc
    later = jnp.where(before, 1.0, 0.0).astype(BF16)
    hs = range(nh)
    pair = lambda h: slice((h // 2) * LANES, (h // 2 + 1) * LANES)
    qh = [q_ref[0, :, pair(h)] * jnp.where(lane // HEAD_DIM == h % 2, 1.0, 0.0).astype(BF16) for h in hs]

    def tile(j, state, diag):
        r0 = pl.multiple_of(j * t, t)
        z = [_dot_nt(k_ref[0, pl.ds(r0, t), pair(h)], qh[h]) for h in hs]
        nz = [-z[h] for h in hs]
        lg = [jnp.log2(1.0 + jnp.exp2(jnp.minimum(z[h], nz[h]))) for h in hs]
        lb = [jnp.minimum(nz[h], 0.0) - lg[h] for h in hs]
        lr = [jnp.where(before, lb[h], 0.0) for h in hs] if diag else lb
        hi = [lr[h].astype(BF16) for h in hs]
        hl = [jnp.concatenate([hi[h], (lr[h] - hi[h].astype(F32)).astype(BF16)], axis=1) for h in hs]
        bt = [_dot(later, hl[h]) for h in hs]
        w = [jnp.exp2(lb[h] + z[h] + (bt[h][:, :t] + bt[h][:, t:] + state[h][0])) for h in hs]
        if diag:
            w = [jnp.where(before, w[h], 0.0) for h in hs]
        pv = [_dot(vt_ref[0, j, h * HEAD_DIM:(h + 1) * HEAD_DIM, :], w[h].astype(BF16)) for h in hs]
        return tuple((state[h][0] + jnp.sum(lr[h], axis=0, keepdims=True), state[h][1] + pv[h]) for h in hs)

    zero = (jnp.zeros((1, t), F32), jnp.zeros((HEAD_DIM, t), F32))
    state = tile(i, (zero,) * nh, True)

    def cond(c):
        return (c[0] >= 0) & (c[1] > -SB_UNDERFLOW)

    def body(c):
        st = tile(c[0], c[2], False)
        top = st[0][0]
        for h in range(1, nh):
            top = jnp.maximum(top, st[h][0])
        return c[0] - 1, jnp.max(top), st

    _, _, state = lax.while_loop(cond, body, (i - 1, jnp.float32(0.0), state))
    for p in range(nh // 2):
        o_ref[0, :, p * LANES:(p + 1) * LANES] = jnp.concatenate(
            [state[2 * p][1], state[2 * p + 1][1]], axis=0).T.astype(o_ref.dtype)


def _sb_attention(qk, vt, nh=SB_HEADS):
    b, s, _ = qk.shape
    t = vt.shape[-1]
    wb = nh * HEAD_DIM
    nblk = SB_WIDTH // wb
    return pl.pallas_call(
        functools.partial(_sb_kernel, t=t, nh=nh),
        out_shape=jax.ShapeDtypeStruct((b, s, SB_WIDTH), BF16),
        grid=(b, nblk, s // t),
        in_specs=[pl.BlockSpec((1, t, wb), lambda bi, hb, i: (bi, i, hb)),
                  pl.BlockSpec((1, s, wb), lambda bi, hb, i: (bi, 0, nblk + hb)),
                  pl.BlockSpec((1, s // t, wb, t), lambda bi, hb, i: (bi, 0, hb, 0))],
        out_specs=pl.BlockSpec((1, t, wb), lambda bi, hb, i: (bi, i, hb)),
        compiler_params=_cparams("parallel", "parallel", "arbitrary"),
        name="sb_attention",
    )(qk, qk, vt)


CONV_PAD = 32


def _proj_conv_kernel(x_ref, w_ref, cw_ref, cb_ref, clg_ref, clb_ref, *rest, tiles_per_seq, sub, n_cast):
    casts_in, (qk_ref, vt_ref, cv_ref), casts_out, hist = (
        rest[:n_cast], rest[n_cast:n_cast + 3], rest[n_cast + 3:2 * n_cast + 3], rest[-1])
    for src_ref, dst_ref in zip(casts_in, casts_out):
        dst_ref[...] = src_ref[...].astype(dst_ref.dtype)
    i = pl.program_id(0)
    tm = x_ref.shape[0]
    xb = x_ref[...].astype(BF16)
    off = CONV_PAD - (CONV_WIDTH - 1)
    prev = jnp.where(i % tiles_per_seq == 0, 0.0, hist[...])
    for r0 in range(0, tm, sub):
        ag = _dot(xb[r0:r0 + sub, :], w_ref[:, 0:2 * CONV_CH])
        if r0 == 0:
            qk_ref[...] = _dot(xb, w_ref[:, 2 * CONV_CH:2 * CONV_CH + 2 * SB_WIDTH]).astype(qk_ref.dtype)
        if r0 == sub:
            vb = _dot(xb, w_ref[:, 2 * CONV_CH + 2 * SB_WIDTH:]).astype(vt_ref.dtype)
            for s in range(vt_ref.shape[0]):
                vt_ref[s] = vb[s * KEY_TILE:(s + 1) * KEY_TILE, :].astype(F32).T.astype(vt_ref.dtype)
        glu = ag[:, :CONV_CH] * _sigmoid(ag[:, CONV_CH:])
        hext = jnp.concatenate([prev, glu], axis=0)
        prev = glu[sub - CONV_PAD:, :]
        acc = jnp.zeros((sub, CONV_CH), F32) + cb_ref[...]
        for ph in range(SUBLANES):
            taps = [w for w in range(CONV_WIDTH) if (off + w) % SUBLANES == ph]
            rows = sub + (SUBLANES if ph else 0)
            part = None
            for w in taps:
                a0 = (off + w) // SUBLANES * SUBLANES
                term = cw_ref[w:w + 1, :] * hext[a0:a0 + rows, :]
                part = term if part is None else part + term
            acc = acc + part[ph:ph + sub, :]
        y = _layer_norm(acc, clg_ref[...], clb_ref[...])
        cv_ref[r0:r0 + sub, :] = (y * _sigmoid(y)).astype(cv_ref.dtype)
    hist[...] = prev


def _proj_conv(x, w, conv_w, conv_b, conv_ln_g, conv_ln_b, seq, casts=(), tm=1024, sub=128):
    m, k = x.shape
    n = w.shape[1]
    steps = m // tm
    assert seq % tm == 0 and tm % sub == 0 and tm >= 2 * sub and sub >= CONV_PAD
    assert all(c.shape[0] % (16 * steps) == 0 for c in casts)
    cvec = lambda v: v.reshape(1, CONV_CH).astype(F32)
    full = lambda shape: pl.BlockSpec(shape, lambda i: (0, 0))
    slab = lambda c: pl.BlockSpec((c.shape[0] // steps, c.shape[1]), lambda i: (i, 0))
    return pl.pallas_call(
        functools.partial(_proj_conv_kernel, tiles_per_seq=seq // tm, sub=sub, n_cast=len(casts)),
        out_shape=[jax.ShapeDtypeStruct((m, 2 * SB_WIDTH), BF16),
                   jax.ShapeDtypeStruct((m // KEY_TILE, SB_WIDTH, KEY_TILE), BF16),
                   jax.ShapeDtypeStruct((m, CONV_CH), BF16)] + [jax.ShapeDtypeStruct(c.shape, BF16) for c in casts],
        grid=(steps,),
        in_specs=[pl.BlockSpec((tm, k), lambda i: (i, 0)), full((k, n)), full((CONV_WIDTH, CONV_CH)),
                  full((1, CONV_CH)), full((1, CONV_CH)), full((1, CONV_CH))] + [slab(c) for c in casts],
        out_specs=[pl.BlockSpec((tm, 2 * SB_WIDTH), lambda i: (i, 0)),
                   pl.BlockSpec((tm // KEY_TILE, SB_WIDTH, KEY_TILE), lambda i: (i, 0, 0)),
                   pl.BlockSpec((tm, CONV_CH), lambda i: (i, 0))] + [slab(c) for c in casts],
        scratch_shapes=[pltpu.VMEM((CONV_PAD, CONV_CH), F32)],
        compiler_params=_cparams("arbitrary"),
        name="proj_conv",
    )(x, w, conv_w.astype(F32), cvec(conv_b), cvec(conv_ln_g), cvec(conv_ln_b), *casts)


FF_CHUNK = 256


def _post_mixer_kernel(l_ref, r_ref, wo_ref, x_ref, g1_ref, b1_ref, wg_ref, wu_ref, wd_ref, g2_ref, b2_ref, o_ref):
    tm, kl = l_ref.shape
    hm = tm // 2
    halves = (slice(0, hm), slice(hm, tm))
    n_chunks = D_FF // FF_CHUNK
    chunk = lambda n: slice(n * FF_CHUNK, (n + 1) * FF_CHUNK)
    m = [_dot(l_ref[rows, :], wo_ref[0:kl, :]) + _dot(r_ref[rows, :], wo_ref[kl:, :]) for rows in halves]
    x, xb, gate, up = [None, None], [None, None], [None, None], [None, None]
    for p, rows in enumerate(halves):
        x[p] = _layer_norm(ALPHA * x_ref[rows, :] + m[p], g1_ref[...], b1_ref[...])
        xb[p] = x[p].astype(BF16)
        gate[p] = _dot(xb[p], wg_ref[:, chunk(0)])
        up[p] = _dot(xb[p], wu_ref[:, chunk(0)])
    xb = jnp.concatenate(xb, axis=0)
    gate = jnp.concatenate(gate, axis=0)
    up = jnp.concatenate(up, axis=0)
    acc = jnp.zeros((tm, x_ref.shape[1]), F32)
    for n in range(n_chunks):
        if n > 0:
            gate = _dot(xb, wg_ref[:, chunk(n)])
            up = _dot(xb, wu_ref[:, chunk(n)])
        h = (gate * _sigmoid(gate) * up).astype(BF16)
        if n < n_chunks - 1:
            acc = acc + _dot(h, wd_ref[chunk(n), :])
    for p, rows in enumerate(halves):
        y = acc[rows, :] + _dot(h[rows, :], wd_ref[chunk(n_chunks - 1), :])
        o_ref[rows, :] = _layer_norm(ALPHA * x[p] + y, g2_ref[...], b2_ref[...])


def _post_mixer(left, right, wo, x, g1, b1, wg, wu, wd, layer, g2, b2, tm=512):
    m, d = x.shape
    kl, kr = left.shape[1], right.shape[1]
    ff = wg.shape[1]
    row = lambda w: pl.BlockSpec((tm, w), lambda i: (i, 0))
    res = lambda shape, blk=0: pl.BlockSpec(shape, lambda i: (blk, 0), pipeline_mode=pl.Buffered(1))
    vec = lambda v: v.reshape(1, d).astype(F32)
    return pl.pallas_call(
        _post_mixer_kernel,
        out_shape=jax.ShapeDtypeStruct((m, d), F32),
        grid=(m // tm,),
        in_specs=[row(kl), row(kr), res((kl + kr, d)), row(d), res((1, d)), res((1, d)),
                  res((d, ff), layer), res((d, ff), layer), res((ff, d), layer), res((1, d)), res((1, d))],
        out_specs=row(d),
        compiler_params=_cparams("parallel"),
        name="post_mixer",
    )(left, right, wo, x, vec(g1), vec(b1), wg, wu, wd, vec(g2), vec(b2))


def _compress_kernel(xk_ref, xv_ref, pos_ref, w1_ref, w2_ref, kc_ref, vct_ref):
    n = xk_ref.shape[1] // CMP_STRIDE
    out = []
    for s, x_ref in enumerate((xk_ref, xv_ref)):
        xs = [x_ref[0, pl.ds(r, n, stride=CMP_STRIDE), :] for r in range(CMP_STRIDE)]
        half = lambda o: jnp.concatenate([x + pos_ref[s, o + r:o + r + 1, :] for r, x in enumerate(xs)], axis=1)
        top = _dot(half(0).astype(BF16), w1_ref[s, 0])
        bot = _dot(half(CMP_STRIDE).astype(BF16), w1_ref[s, 1])
        h = top + pltpu.roll(bot, n - 1, 0)
        out.append(_dot(_gelu_tanh(h).astype(BF16), w2_ref[s]))
    kc_ref[0] = out[0].astype(kc_ref.dtype)
    vct_ref[0] = out[1].astype(vct_ref.dtype).astype(F32).T.astype(vct_ref.dtype)


def _block_diag2(a):
    z = jnp.zeros_like(a)
    return jnp.concatenate([jnp.concatenate([a, z], axis=-1), jnp.concatenate([z, a], axis=-1)], axis=-2)


def _compress(cv, pos_k, pos_v, w1k, w1v, w2k, w2v):
    b, s, _ = cv.shape
    n = s // CMP_STRIDE
    pos = jnp.stack([jnp.concatenate([p, p], axis=-1) for p in (pos_k, pos_v)]).astype(F32)
    w1 = jnp.stack([_block_diag2(w.reshape(CMP_BLOCK, HEAD_DIM, HEAD_DIM)) for w in (w1k, w1v)])
    w1 = w1.reshape(2, 2, CMP_STRIDE * KV_WIDTH, KV_WIDTH).astype(BF16)
    w2 = jnp.stack([_block_diag2(w) for w in (w2k, w2v)]).astype(BF16)
    full = lambda a: pl.BlockSpec(a.shape, lambda bi: (0,) * a.ndim)
    return pl.pallas_call(
        _compress_kernel,
        out_shape=[jax.ShapeDtypeStruct((b, n, KV_WIDTH), BF16), jax.ShapeDtypeStruct((b, KV_WIDTH, n), BF16)],
        grid=(b,),
        in_specs=[pl.BlockSpec((1, s, KV_WIDTH), lambda bi: (bi, 0, 0)),
                  pl.BlockSpec((1, s, KV_WIDTH), lambda bi: (bi, 0, 1)), full(pos), full(w1), full(w2)],
        out_specs=[pl.BlockSpec((1, n, KV_WIDTH), lambda bi: (bi, 0, 0)),
                   pl.BlockSpec((1, KV_WIDTH, n), lambda bi: (bi, 0, 0))],
        compiler_params=_cparams("parallel"),
        name="nsa_compress",
    )(cv, cv, pos, w1, w2)


NSA_GROUP = 4


def _nsa_kernel(q_ref, ks_ref, vst_ref, kw_ref, vwt_ref, kc_ref, vct_ref, gt_ref, o_ref, *, t, n_cmp):
    i = pl.program_id(1)
    nh = NSA_GROUP
    lane = lax.broadcasted_iota(jnp.int32, (1, LANES), 1)
    tpos = i * t + lax.broadcasted_iota(jnp.int32, (1, t), 1)
    krow = lax.broadcasted_iota(jnp.int32, (t, 1), 0)
    qg = []
    for g in range(2):
        hm = jnp.where(lane // HEAD_DIM == g, 1.0, 0.0).astype(BF16)
        qg.append(jnp.concatenate([q_ref[0, :, m * LANES:(m + 1) * LANES] * hm for m in range(nh)], axis=0))

    nblk = lax.broadcasted_iota(jnp.int32, (LANES, 1), 0)
    cvalid = (nblk * CMP_STRIDE + (CMP_BLOCK - 1) <= tpos) & (nblk < n_cmp)
    jr = lax.broadcasted_iota(jnp.int32, (SEL_LANES, LANES), 0)
    nc = lax.broadcasted_iota(jnp.int32, (SEL_LANES, LANES), 1)
    ovl_t = jnp.where((nc * CMP_STRIDE < jr * SEL_BLOCK + SEL_BLOCK) & (nc * CMP_STRIDE + CMP_BLOCK > jr * SEL_BLOCK)
                      & (nc < n_cmp), 1.0, 0.0).astype(BF16)
    jsel = lax.broadcasted_iota(jnp.int32, (SEL_LANES, 1), 0)
    blk_t = tpos // SEL_BLOCK
    forced = (jsel == 0) | (jsel == blk_t) | (jsel == blk_t - 1)
    pad_rows = lax.broadcasted_iota(jnp.int32, (LANES - SEL_LANES, 1), 0)
    o_cmp = []
    qx = []
    for g in range(2):
        st = _dot_nt(kc_ref[0], qg[g])
        ps = []
        psum = jnp.zeros((LANES, t), F32)
        for m in range(nh):
            s = jnp.where(cvalid, st[:, m * t:(m + 1) * t], NEG)
            e = jnp.where(cvalid, jnp.exp2(s - jnp.max(s, axis=0, keepdims=True)), 0.0)
            l = jnp.sum(e, axis=0, keepdims=True)
            p = e * (1.0 / jnp.where(l > 0.0, l, 1.0))
            ps.append(p.astype(BF16))
            psum = psum + p
        o_cmp.append(_dot(vct_ref[0, g * HEAD_DIM:(g + 1) * HEAD_DIM, :], jnp.concatenate(ps, axis=1)))
        hi = psum.astype(BF16)
        lo = (psum - hi.astype(F32)).astype(BF16)
        it = _dot(ovl_t, jnp.concatenate([hi, lo], axis=1))
        imp = it[:, :t] + it[:, t:]
        score = jnp.where(jsel <= blk_t, imp + jnp.where(forced, FORCE_BONUS, 0.0), NEG)
        cnt = jnp.zeros((SEL_LANES, t), F32)
        for c in range(SEL_LANES):
            row = score[c:c + 1, :]
            cnt = cnt + jnp.where((row > score) | ((row == score) & (jsel > c)), 1.0, 0.0)
        tail = jnp.where(pad_rows == NSA_BIAS_LANE - SEL_LANES, NEG, 0.0) + jnp.zeros((1, t), F32)
        selt = jnp.concatenate([jnp.where(cnt < SEL_TOPK, 0.0, NEG), tail], axis=0).T.astype(BF16)
        qx.append(jnp.concatenate([qg[g], jnp.concatenate([selt] * nh, axis=0)], axis=1))

    def attend(streams, scores=None):
        if scores is None:
            scores = [_dot_nt(k, q) for k, _, q, _, _ in streams]
        sb = [s if bias is None else s + jnp.concatenate([bias] * nh, axis=1)
              for s, (_, _, _, bias, _) in zip(scores, streams)]
        mnew = [jnp.maximum(st[0], jnp.max(s, axis=0, keepdims=True)) for s, (_, _, _, _, st) in zip(sb, streams)]
        p = [jnp.exp2(s - mn) for s, mn in zip(sb, mnew)]
        a = [jnp.exp2(st[0] - mn) for mn, (_, _, _, _, st) in zip(mnew, streams)]
        pv = [_dot(vt, pp.astype(BF16)) for pp, (_, vt, _, _, _) in zip(p, streams)]
        return [(mn, aa * st[1] + jnp.sum(pp, axis=0, keepdims=True), aa * st[2] + o)
                for mn, aa, pp, o, (_, _, _, _, st) in zip(mnew, a, p, pv, streams)]

    init = (jnp.full((1, nh * t), NEG, F32), jnp.zeros((1, nh * t), F32), jnp.zeros((HEAD_DIM, nh * t), F32))
    per_tile = t // SEL_BLOCK
    rows_g = lambda g: slice(g * HEAD_DIM, (g + 1) * HEAD_DIM)

    def slc_streams(j, jc, state, bias):
        onehot = (lane == j * per_tile + krow // SEL_BLOCK) | ((lane == NSA_BIAS_LANE) & (j < 0))
        r0 = pl.multiple_of(jc * t, t)
        kx = jnp.concatenate([ks_ref[0, pl.ds(r0, t), :], jnp.where(onehot, 1.0, 0.0).astype(BF16)], axis=1)
        return [(kx, vst_ref[0, jc, rows_g(g), :], qx[g], bias, state[g]) for g in range(2)]

    slc = [init, init]
    win = [init, init]
    n_win = WINDOW // t + 1
    for d in range(n_win):
        j = i - d
        jc = jnp.maximum(j, 0)
        kpos = j * t + krow
        wbias = jnp.where((kpos >= 0) & (kpos <= tpos) & (tpos - kpos < WINDOW), 0.0, NEG)
        causal = jnp.where(kpos <= tpos, 0.0, NEG) if d == 0 else None
        r0 = pl.multiple_of(jc * t, t)
        sstreams = slc_streams(j, jc, slc, causal)
        wstreams = [(None, vwt_ref[0, jc, rows_g(g), :], None, wbias, win[g]) for g in range(2)]
        kwx = jnp.concatenate([kw_ref[0, pl.ds(r0, t), :], jnp.zeros((t, LANES), BF16)], axis=1)
        both = [_dot_nt(jnp.concatenate([sstreams[g][0], kwx], axis=0), qx[g]) for g in range(2)]
        res = attend(sstreams + wstreams, [both[0][:t], both[1][:t], both[0][t:], both[1][t:]])
        slc, win = res[:2], res[2:]

    n_rest = jnp.maximum(i + 1 - n_win, 0)

    def pair_body(p, state):
        j = i - n_win - 2 * p
        hi_s, lo_s = slc_streams(j, j, state, None), slc_streams(j - 1, j - 1, state, None)
        return tuple(attend([(jnp.concatenate([a[0], b[0]], axis=0), jnp.concatenate([a[1], b[1]], axis=1), a[2], None, a[4])
                             for a, b in zip(hi_s, lo_s)]))

    slc = lax.fori_loop(0, n_rest // 2, pair_body, tuple(slc))
    slc = lax.fori_loop(0, n_rest % 2, lambda _, state: tuple(attend(slc_streams(0, 0, state, None))), slc)

    gates = _sigmoid(gt_ref[0]).T
    for m in range(nh):
        halves = []
        for g in range(2):
            c = g * 3 * nh + m * 3
            sl = slice(m * t, (m + 1) * t)
            o_s = slc[g][2][:, sl] * (1.0 / slc[g][1][:, sl])
            o_w = win[g][2][:, sl] * (1.0 / win[g][1][:, sl])
            halves.append(gates[c:c + 1, :] * o_cmp[g][:, sl] + gates[c + 1:c + 2, :] * o_s + gates[c + 2:c + 3, :] * o_w)
        o_ref[0, :, m * LANES:(m + 1) * LANES] = jnp.concatenate(halves, axis=0).T.astype(o_ref.dtype)


def _nsa_attention(qk, vt, kc2, vct, gt, n_cmp):
    b, s, _ = qk.shape
    t = vt.shape[-1]
    assert WINDOW % t == 0 and t % SEL_BLOCK == 0
    nt = s // t
    kspec = lambda col: pl.BlockSpec((1, s, LANES), lambda bi, i: (bi, 0, col))
    vspec = lambda blk: pl.BlockSpec((1, nt, LANES, t), lambda bi, i: (bi, 0, blk, 0))
    cspec = pl.BlockSpec((1, LANES, LANES), lambda bi, i: (bi, 0, 0))
    return pl.pallas_call(
        functools.partial(_nsa_kernel, t=t, n_cmp=n_cmp),
        out_shape=jax.ShapeDtypeStruct((b, s, NSA_WIDTH), BF16),
        grid=(b, nt),
        in_specs=[pl.BlockSpec((1, t, NSA_WIDTH), lambda bi, i: (bi, i, 0)),
                  kspec(4), vspec(0), kspec(5), vspec(1), cspec, cspec,
                  pl.BlockSpec((1, t, LANES), lambda bi, i: (bi, i, 0))],
        out_specs=pl.BlockSpec((1, t, NSA_WIDTH), lambda bi, i: (bi, i, 0)),
        compiler_params=_cparams("parallel", "arbitrary"),
        name="nsa_attention",
    )(qk, qk, vt, qk, vt, kc2, vct, gt)


def _proj_gmlp_kernel(x_ref, w_ref, lg_ref, lb_ref, ws_ref, bs_ref, qk_ref, cv_ref, gt_ref, vt_ref, mlp_ref):
    xb = x_ref[...].astype(BF16)
    tm = x_ref.shape[0]
    c1 = 2 * GMLP_WIDTH
    uv = _dot(xb, w_ref[:, 0:c1])
    qk_ref[...] = _dot(xb, w_ref[:, c1:c1 + 768]).astype(qk_ref.dtype)
    cv_ref[...] = _dot(xb, w_ref[:, c1 + 768:c1 + 1024])
    gt_ref[...] = _dot(xb, w_ref[:, c1 + 1024:c1 + 1024 + LANES])
    vb = _dot(xb, w_ref[:, c1 + 1024 + LANES:]).astype(vt_ref.dtype)
    for s in range(vt_ref.shape[0]):
        vt_ref[s] = vb[s * KEY_TILE:(s + 1) * KEY_TILE, :].astype(F32).T.astype(vt_ref.dtype)
    cg = GMLP_WIDTH // GMLP_GROUPS
    u = _gelu_tanh(uv[:, :GMLP_WIDTH])
    v = _layer_norm(_gelu_tanh(uv[:, GMLP_WIDTH:]), lg_ref[...], lb_ref[...]).astype(BF16)
    rr = lax.broadcasted_iota(jnp.int32, (GMLP_CHUNK, GMLP_CHUNK), 0)
    cc = lax.broadcasted_iota(jnp.int32, (GMLP_CHUNK, GMLP_CHUNK), 1)
    for g in range(GMLP_GROUPS):
        wm = jnp.where(rr >= cc, ws_ref[g], 0.0).astype(BF16)
        bias = bs_ref[:, g:g + 1]
        for c0 in range(0, tm, GMLP_CHUNK):
            mixed = _dot(wm, v[c0:c0 + GMLP_CHUNK, g * cg:(g + 1) * cg]) + bias
            mlp_ref[c0:c0 + GMLP_CHUNK, g * cg:(g + 1) * cg] = (
                u[c0:c0 + GMLP_CHUNK, g * cg:(g + 1) * cg] * mixed).astype(mlp_ref.dtype)


def _proj_gmlp(x, w, ln_g, ln_b, ws, bs, tm=1024):
    m, k = x.shape
    assert tm % GMLP_CHUNK == 0
    vec = lambda a: a.reshape(1, GMLP_WIDTH).astype(F32)
    full = lambda a: pl.BlockSpec(a.shape, lambda i: (0,) * a.ndim)
    row = lambda wd: pl.BlockSpec((tm, wd), lambda i: (i, 0))
    args = (x, w, vec(ln_g), vec(ln_b), ws.astype(F32), jnp.transpose(bs).astype(F32))
    return pl.pallas_call(
        _proj_gmlp_kernel,
        out_shape=[jax.ShapeDtypeStruct((m, 768), BF16), jax.ShapeDtypeStruct((m, 2 * KV_WIDTH), F32),
                   jax.ShapeDtypeStruct((m, LANES), F32),
                   jax.ShapeDtypeStruct((m // KEY_TILE, 2 * KV_WIDTH, KEY_TILE), BF16),
                   jax.ShapeDtypeStruct((m, GMLP_WIDTH), BF16)],
        grid=(m // tm,),
        in_specs=[row(k)] + [full(a) for a in args[1:]],
        out_specs=[row(768), row(2 * KV_WIDTH), row(LANES),
                   pl.BlockSpec((tm // KEY_TILE, 2 * KV_WIDTH, KEY_TILE), lambda i: (i, 0, 0)), row(GMLP_WIDTH)],
        compiler_params=_cparams("parallel"),
        name="proj_gmlp",
    )(*args)


def _head_pair_perm():
    m, g, d = np.meshgrid(np.arange(4), np.arange(2), np.arange(HEAD_DIM), indexing="ij")
    return ((g * 4 + m) * HEAD_DIM + d).reshape(-1)


def _even_layer(x, w_in, conv_w, conv_b, conv_ln_g, conv_ln_b, w_out, post, casts, b, s):
    w_sc = jnp.concatenate([w_in[:, :SB_WIDTH] * (QK_SCALE * LOG2E), w_in[:, SB_WIDTH:]], axis=1)
    w_sc = jnp.concatenate([w_sc[:, 3 * SB_WIDTH:], w_sc[:, :3 * SB_WIDTH]], axis=1)
    qk, vt, o_cv, *cast = _proj_conv(x, w_sc.astype(BF16), conv_w, conv_b, conv_ln_g, conv_ln_b, s, casts)
    o_sb = _sb_attention(qk.reshape(b, s, -1), vt.reshape(b, s // KEY_TILE, SB_WIDTH, KEY_TILE))
    return _post_mixer(o_sb.reshape(b * s, -1), o_cv, w_out.astype(BF16), x, *post(cast)), cast


def _odd_layer(x, w_in, cmpk_pos, cmpk_w1, cmpk_w2, cmpv_pos, cmpv_w1, cmpv_w2,
               gmlp_ln_g, gmlp_ln_b, gmlp_ws, gmlp_bs, w_out, post, b, s):
    perm = _head_pair_perm()
    o_gt, o_u = NSA_WIDTH + 6 * KV_WIDTH, NSA_WIDTH + 6 * KV_WIDTH + 24
    w_gt = jnp.pad(w_in[:, o_gt:o_u], ((0, 0), (0, LANES - 24)))
    col = lambda n: w_in[:, NSA_WIDTH + n * KV_WIDTH:NSA_WIDTH + (n + 1) * KV_WIDTH]
    w_re = jnp.concatenate([w_in[:, o_u:], w_in[:, perm] * (QK_SCALE * LOG2E), col(2), col(4), col(0), col(1), w_gt,
                            col(3), col(5)], axis=1)
    qk, cv, gt, vt, o_mlp = _proj_gmlp(x, w_re.astype(BF16), gmlp_ln_g, gmlp_ln_b, gmlp_ws, gmlp_bs)
    n_cmp = (s - CMP_BLOCK) // CMP_STRIDE + 1
    kc2, vct = _compress(cv.reshape(b, s, -1), cmpk_pos, cmpv_pos, cmpk_w1, cmpv_w1, cmpk_w2, cmpv_w2)
    o_nsa = _nsa_attention(qk.reshape(b, s, -1), vt.reshape(b, s // KEY_TILE, 2 * KV_WIDTH, KEY_TILE), kc2, vct,
                           gt.reshape(b, s, -1), n_cmp)
    w_out_re = jnp.concatenate([w_out[:NSA_WIDTH][perm], w_out[NSA_WIDTH:]], axis=0)
    return _post_mixer(o_nsa.reshape(b * s, -1), o_mlp, w_out_re.astype(BF16), x, *post)


def kernel(x, ev_w_in, ev_conv_w, ev_conv_b, ev_conv_ln_g, ev_conv_ln_b, ev_w_out, od_w_in, od_cmpk_pos, od_cmpk_w1, od_cmpk_w2, od_cmpv_pos, od_cmpv_w1, od_cmpv_w2, od_gmlp_ln_g, od_gmlp_ln_b, od_gmlp_ws, od_gmlp_bs, od_w_out, ffn_w_gate, ffn_w_up, ffn_w_down, ln1_g, ln1_b, ln2_g, ln2_b):
    b, s, d = x.shape
    assert DEPTH == 2 and s // CMP_STRIDE == LANES and s // SEL_BLOCK == SEL_LANES
    h = x.reshape(b * s, d)
    ff = ffn_w_gate.shape[-1]
    casts = (ffn_w_gate.reshape(DEPTH * d, ff), ffn_w_up.reshape(DEPTH * d, ff), ffn_w_down.reshape(DEPTH * ff, d))
    post = lambda layer, w: (ln1_g[layer], ln1_b[layer], *w, layer, ln2_g[layer], ln2_b[layer])
    h, ffn_w = _even_layer(h, ev_w_in[0], ev_conv_w[0], ev_conv_b[0], ev_conv_ln_g[0], ev_conv_ln_b[0], ev_w_out[0],
                           functools.partial(post, 0), casts, b, s)
    h = _odd_layer(h, od_w_in[0], od_cmpk_pos[0], od_cmpk_w1[0], od_cmpk_w2[0], od_cmpv_pos[0], od_cmpv_w1[0],
                   od_cmpv_w2[0], od_gmlp_ln_g[0], od_gmlp_ln_b[0], od_gmlp_ws[0], od_gmlp_bs[0], od_w_out[0],
                   post(1, ffn_w), b, s)
    return h.reshape(b, s, d)
```
